```python
import math
import jax
import jax.numpy as jnp
from jax import lax
import numpy as np

D_MODEL = 2048
BATCH = 8
SEQ = 2048
DEPTH = 4

CTX_LEN = 256
GRID_W = 64
N_MIXERS = 4
MIXER_RET = 0
MIXER_WIN = 1
MIXER_GDN = 2
MIXER_SGU = 3
NORM_EPS = 1e-6
ROPE_BASE = 10000.0
MLP_HIDDEN = 4 * D_MODEL

RET_HEADS = 8
RET_DK = D_MODEL // RET_HEADS
RET_DV = 2 * RET_DK
RET_CHUNK = 128

ATT_HEADS = 16
ATT_KV_HEADS = 4
ATT_HD = D_MODEL // ATT_HEADS
ATT_GROUP = ATT_HEADS // ATT_KV_HEADS
WINDOW = 128
ATT_BLOCK = 128

GDN_K_HEADS = 16
GDN_V_HEADS = 32
GDN_DK = 128
GDN_DV = 128
GDN_CONV = 4
GDN_CONV_LEFT = GDN_CONV // 2
GDN_CONV_RIGHT = GDN_CONV - 1 - GDN_CONV_LEFT
GDN_CHUNK = 64

SGU_CHUNK = 128
SGU_GROUPS = 8
SGU_WIDTH = 2 * D_MODEL

kernel_name = 'hybrid_interleaved_diffusion_trunk'


def rms_norm(x, g):
    xf = x.astype(jnp.float32)
    y = xf * lax.rsqrt(jnp.mean(xf * xf, axis=-1, keepdims=True) + NORM_EPS)
    return (y * g.astype(jnp.float32)).astype(x.dtype)


def head_rms(x):
    xf = x.astype(jnp.float32)
    return xf * lax.rsqrt(jnp.mean(xf * xf, axis=-1, keepdims=True) + NORM_EPS)


def layer_norm(x, g, b):
    xf = x.astype(jnp.float32)
    mu = jnp.mean(xf, axis=-1, keepdims=True)
    var = jnp.mean(jnp.square(xf - mu), axis=-1, keepdims=True)
    y = (xf - mu) * lax.rsqrt(var + NORM_EPS) * g.astype(jnp.float32) + b.astype(jnp.float32)
    return y.astype(x.dtype)


def l2_normalize(x):
    xf = x.astype(jnp.float32)
    return (xf * lax.rsqrt(jnp.sum(xf * xf, axis=-1, keepdims=True) + 1e-6)).astype(x.dtype)


def flip_seq(t):
    return jnp.flip(t, axis=2)


def axial_rope_tables(rows, head_dim):
    row = jnp.repeat(jnp.arange(rows, dtype=jnp.float32), GRID_W)
    col = jnp.tile(jnp.arange(GRID_W, dtype=jnp.float32), rows)
    axis_dim = head_dim // 2
    inv_freq = jnp.exp(-math.log(ROPE_BASE) * jnp.arange(0, axis_dim, 2, dtype=jnp.float32) / axis_dim)
    ang = jnp.concatenate([row[:, None] * inv_freq, col[:, None] * inv_freq], axis=-1)
    return jnp.cos(ang), jnp.sin(ang)


def apply_rope(x, cos, sin):
    xf = x.astype(jnp.float32)
    x1 = xf[..., 0::2]
    x2 = xf[..., 1::2]
    cs = cos[None, :, None, :]
    sn = sin[None, :, None, :]
    y = jnp.stack([x1 * cs - x2 * sn, x1 * sn + x2 * cs], axis=-1).reshape(x.shape)
    return y.astype(x.dtype)


def ada_params(cond, w, b):
    m = jax.nn.silu(cond) @ w + b
    return jnp.split(m[:, None, :], 6, axis=-1)


def modulate(h, shift, scale):
    return h * (1.0 + scale) + shift


def sq_relu_mlp(h, w_up, w_down):
    a = jax.nn.relu(h @ w_up)
    return (a * a) @ w_down


def ctx_feeds_later(i):
    return any(k % N_MIXERS != MIXER_SGU for k in range(i + 1, DEPTH))


def retention_chunked(q, k, v, s0):
    b, h, s, _ = q.shape
    dv = v.shape[-1]
    n = s // RET_CHUNK
    lg = jnp.log1p(-jnp.exp2(-5.0 - jnp.arange(RET_HEADS, dtype=jnp.float32)))[:, None]
    pos = jnp.arange(RET_CHUNK, dtype=jnp.float32)
    diff = pos[:, None] - pos[None, :]
    intra = jnp.where(diff >= 0, jnp.exp(lg[:, :, None] * jnp.maximum(diff, 0.0)), 0.0)
    q_decay = jnp.exp(lg * (pos + 1.0))[:, :, None]
    k_decay = jnp.exp(lg * (RET_CHUNK - 1.0 - pos))[:, :, None]
    c_decay = jnp.exp(lg * RET_CHUNK)[:, :, None]

    def chunks(t):
        return jnp.moveaxis(t.astype(jnp.float32).reshape(b, h, n, RET_CHUNK, t.shape[-1]), 2, 0)

    def step(state, inp):
        qi, ki, vi = inp
        scores = jnp.einsum('bhcd,bhmd->bhcm', qi, ki) * intra
        out = (jnp.einsum('bhcm,bhmv->bhcv', scores, vi)
               + jnp.einsum('bhcd,bhdv->bhcv', qi * q_decay, state))
        state = state * c_decay + jnp.einsum('bhcd,bhcv->bhdv', ki * k_decay, vi)
        return state, out

    state, out = lax.scan(step, s0, (chunks(q), chunks(k), chunks(v)))
    return jnp.moveaxis(out, 0, 2).reshape(b, h, s, dv), state


def retention_mixer(h_lat, h_ctx, w_in, gn_g, w_out, cos, sin, want_ctx):
    qk_w = RET_HEADS * RET_DK
    v_w = RET_HEADS * RET_DV

    def project(h, rotate):
        bsz, n, _ = h.shape
        q, k, v, gate = jnp.split(h @ w_in, [qk_w, 2 * qk_w, 2 * qk_w + v_w], axis=-1)
        q = q.reshape(bsz, n, RET_HEADS, RET_DK)
        k = k.reshape(bsz, n, RET_HEADS, RET_DK) * RET_DK ** -0.5
        if rotate:
            q = apply_rope(q, cos, sin)
            k = apply_rope(k, cos, sin)
        v = v.reshape(bsz, n, RET_HEADS, RET_DV)
        return q.transpose(0, 2, 1, 3), k.transpose(0, 2, 1, 3), v.transpose(0, 2, 1, 3), gate

    def finish(o, gate):
        bsz, _, n, _ = o.shape
        o = head_rms(o.transpose(0, 2, 1, 3)).reshape(bsz, n, v_w) * gn_g.astype(jnp.float32)
        return (o.astype(gate.dtype) * jax.nn.silu(gate)) @ w_out

    qc, kc, vc, gc = project(h_ctx, False)
    zero = jnp.zeros((qc.shape[0], RET_HEADS, RET_DK, RET_DV), jnp.float32)
    oc_f, sc_f = retention_chunked(qc, kc, vc, zero)
    oc_b, sc_b = retention_chunked(flip_seq(qc), flip_seq(kc), flip_seq(vc), zero)
    ql, kl, vl, gl = project(h_lat, True)
    ol_f, _ = retention_chunked(ql, kl, vl, sc_f)
    ol_b, _ = retention_chunked(flip_seq(ql), flip_seq(kl), flip_seq(vl), sc_b)
    out_lat = finish(ol_f + flip_seq(ol_b), gl)
    out_ctx = finish(oc_f + flip_seq(oc_b), gc) if want_ctx else None
    return out_lat, out_ctx


def window_attention_mixer(h_lat, h_ctx, w_in, sink, w_out, cos, sin, want_ctx):
    q_w = ATT_HEADS * ATT_HD
    kv_w = ATT_KV_HEADS * ATT_HD
    scale = ATT_HD ** -0.5
    sink_l = sink.astype(jnp.float32).reshape(ATT_KV_HEADS, ATT_GROUP)

    def project(h, rotate):
        bsz, n, _ = h.shape
        q, k, v = jnp.split(h @ w_in, [q_w, q_w + kv_w], axis=-1)
        q = q.reshape(bsz, n, ATT_HEADS, ATT_HD)
        k = k.reshape(bsz, n, ATT_KV_HEADS, ATT_HD)
        v = v.reshape(bsz, n, ATT_KV_HEADS, ATT_HD)
        if rotate:
            q = apply_rope(q, cos, sin)
            k = apply_rope(k, cos, sin)
        return q.reshape(bsz, n, ATT_KV_HEADS, ATT_GROUP, ATT_HD) * scale, k, v

    def softmax_with_sink(logits):
        sink_col = jnp.broadcast_to(sink_l[None, :, :, None, None], logits.shape[:-1] + (1,))
        p = jax.nn.softmax(jnp.concatenate([logits, sink_col], axis=-1), axis=-1)
        return p[..., :-1]

    qc, kc, vc = project(h_ctx, False)
    ql, kl, vl = project(h_lat, True)
    bsz, n = ql.shape[0], ql.shape[1]
    n_blocks = n // ATT_BLOCK
    span = ATT_BLOCK + 2 * WINDOW
    pad = ((0, 0), (WINDOW, WINDOW), (0, 0), (0, 0))
    kp = jnp.pad(kl, pad)
    vp = jnp.pad(vl, pad)

    def block(bi):
        start = bi * ATT_BLOCK
        qb = lax.dynamic_slice_in_dim(ql, start, ATT_BLOCK, axis=1)
        kb = lax.dynamic_slice_in_dim(kp, start, span, axis=1)
        vb = lax.dynamic_slice_in_dim(vp, start, span, axis=1)
        qpos = start + jnp.arange(ATT_BLOCK)
        kpos = start - WINDOW + jnp.arange(span)
        valid = ((jnp.abs(qpos[:, None] - kpos[None, :]) <= WINDOW)
                 & (kpos >= 0)[None, :] & (kpos < n)[None, :])
        s_lat = jnp.einsum('bqkgd,bmkd->bkgqm', qb, kb).astype(jnp.float32)
        s_lat = jnp.where(valid, s_lat, -jnp.inf)
        s_ctx = jnp.einsum('bqkgd,bmkd->bkgqm', qb, kc).astype(jnp.float32)
        p = softmax_with_sink(jnp.concatenate([s_lat, s_ctx], axis=-1)).astype(vb.dtype)
        return (jnp.einsum('bkgqm,bmkd->bqkgd', p[..., :span], vb)
                + jnp.einsum('bkgqm,bmkd->bqkgd', p[..., span:], vc))

    o = lax.map(block, jnp.arange(n_blocks))
    o = jnp.moveaxis(o, 0, 1).reshape(bsz, n, q_w)
    out_lat = o @ w_out
    out_ctx = None
    if want_ctx:
        s = jnp.einsum('bqkgd,bmkd->bkgqm', qc, kc).astype(jnp.float32)
        p = softmax_with_sink(s).astype(vc.dtype)
        oc = jnp.einsum('bkgqm,bmkd->bqkgd', p, vc).reshape(bsz, qc.shape[1], q_w)
        out_ctx = oc @ w_out
    return out_lat, out_ctx


def short_conv(x, w):
    return lax.conv_general_dilated(
        x, w[:, None, :].astype(x.dtype), window_strides=(1,),
        padding=[(GDN_CONV_LEFT, GDN_CONV_RIGHT)],
        dimension_numbers=('NWC', 'WIO', 'NWC'), feature_group_count=x.shape[-1])


def gated_delta_chunked(q, k, v, g, beta, s0):
    b, h, s, _ = q.shape
    dv = v.shape[-1]
    C = GDN_CHUNK
    n = s // C
    f32 = jnp.float32
    q, k, v = [t.astype(f32).reshape(b, h, n, C, t.shape[-1]) for t in (q, k, v)]
    beta = beta.astype(f32).reshape(b, h, n, C, 1)
    gc = jnp.cumsum(g.astype(f32).reshape(b, h, n, C), axis=-1)
    idx = jnp.arange(C)
    lower = idx[:, None] >= idx[None, :]
    strict = idx[:, None] > idx[None, :]
    decay = jnp.exp(jnp.where(lower, gc[..., :, None] - gc[..., None, :], -jnp.inf))
    kb = k * beta
    lmat = jnp.where(strict, jnp.einsum('bhncd,bhnmd->bhncm', kb, k) * decay, 0.0)
    eye = jnp.eye(C, dtype=f32)
    tmat = lax.linalg.triangular_solve(lmat + eye, jnp.broadcast_to(eye, lmat.shape),
                                       left_side=True, lower=True, unit_diagonal=True)
    u = jnp.einsum('bhncm,bhnmv->bhncv', tmat, v * beta)
    w = jnp.einsum('bhncm,bhnmd->bhncd', tmat, kb * jnp.exp(gc)[..., None])

    def step(state, inp):
        qi, ki, ui, wi, gi, di = inp
        v_new = ui - jnp.einsum('bhcd,bhdv->bhcv', wi, state)
        attn = jnp.einsum('bhcd,bhmd->bhcm', qi, ki) * di
        out = (jnp.einsum('bhcd,bhdv->bhcv', qi * jnp.exp(gi)[..., None], state)
               + jnp.einsum('bhcm,bhmv->bhcv', attn, v_new))
        g_last = gi[..., -1:]
        state = (state * jnp.exp(g_last)[..., None]
                 + jnp.einsum('bhcd,bhcv->bhdv', ki * jnp.exp(g_last - gi)[..., None], v_new))
        return state, out

    xs = tuple(jnp.moveaxis(t, 2, 0) for t in (q, k, u, w, gc, decay))
    state, out = lax.scan(step, s0.astype(f32), xs)
    return jnp.moveaxis(out, 0, 2).reshape(b, h, s, dv), state


def gdn_mixer(h_lat, h_ctx, w_in, conv_w, a_log, dt_bias, norm_g, w_out, want_ctx):
    qk_w = GDN_K_HEADS * GDN_DK
    v_w = GDN_V_HEADS * GDN_DV
    conv_ch = 2 * qk_w + v_w
    rep = GDN_V_HEADS // GDN_K_HEADS
    f32 = jnp.float32

    def project(h):
        bsz, n, _ = h.shape
        mixed, z, b_raw, a_raw = jnp.split(
            h @ w_in, [conv_ch, conv_ch + v_w, conv_ch + v_w + 2 * GDN_V_HEADS], axis=-1)
        mixed = jax.nn.silu(short_conv(mixed, conv_w))
        q, k, v = jnp.split(mixed, [qk_w, 2 * qk_w], axis=-1)

        def heads(t, nh, d):
            return t.reshape(bsz, n, nh, d).transpose(0, 2, 1, 3)

        q = jnp.repeat(l2_normalize(heads(q, GDN_K_HEADS, GDN_DK)), rep, axis=1) * GDN_DK ** -0.5
        k = jnp.repeat(l2_normalize(heads(k, GDN_K_HEADS, GDN_DK)), rep, axis=1)
        v = heads(v, GDN_V_HEADS, GDN_DV)
        beta = jax.nn.sigmoid(b_raw.astype(f32)).reshape(bsz, n, 2, GDN_V_HEADS).transpose(2, 0, 3, 1)
        a_dir = a_raw.astype(f32).reshape(bsz, n, 2, GDN_V_HEADS).transpose(2, 0, 3, 1)
        g = -jnp.exp(a_log.astype(f32))[:, None, :, None] * jax.nn.softplus(
            a_dir + dt_bias.astype(f32)[:, None, :, None])
        return q, k, v, beta, g, z

    def bidir(q, k, v, beta, g, s_f, s_b):
        o_f, s_f = gated_delta_chunked(q, k, v, g[0], beta[0], s_f)
        o_b, s_b = gated_delta_chunked(flip_seq(q), flip_seq(k), flip_seq(v),
                                       flip_seq(g[1]), flip_seq(beta[1]), s_b)
        return o_f + flip_seq(o_b), s_f, s_b

    def finish(o, z):
        bsz, _, n, _ = o.shape
        o = head_rms(o.transpose(0, 2, 1, 3)) * norm_g.astype(f32)
        o = o.astype(z.dtype) * jax.nn.silu(z.reshape(bsz, n, GDN_V_HEADS, GDN_DV))
        return o.reshape(bsz, n, v_w) @ w_out

    qc, kc, vc, bc, gcx, zc = project(h_ctx)
    zero = jnp.zeros((qc.shape[0], GDN_V_HEADS, GDN_DK, GDN_DV), f32)
    oc, sc_f, sc_b = bidir(qc, kc, vc, bc, gcx, zero, zero)
    ql, kl, vl, bl, gl, zl = project(h_lat)
    ol, _, _ = bidir(ql, kl, vl, bl, gl, sc_f, sc_b)
    out_lat = finish(ol, zl)
    out_ctx = finish(oc, zc) if want_ctx else None
    return out_lat, out_ctx


def sgu_mixer(h, w_in, ln_g, ln_b, w_s, b_s, w_out):
    bsz, n, _ = h.shape
    nch = n // SGU_CHUNK
    z = jax.nn.gelu(h @ w_in, approximate=False)
    u, v = jnp.split(z, 2, axis=-1)
    v = layer_norm(v, ln_g, ln_b)
    v = v.reshape(bsz, nch, SGU_CHUNK, SGU_GROUPS, SGU_WIDTH // SGU_GROUPS)
    v = jnp.einsum('gpq,bnqgc->bnpgc', w_s, v) + b_s.T[None, None, :, :, None]
    return (u * v.reshape(bsz, n, SGU_WIDTH)) @ w_out


def setup_inputs(seed: int = 0) -> dict:
    key = jax.random.key(seed)
    ks = iter(jax.random.split(key, 40))
    f32 = jnp.float32
    D = D_MODEL

    def normal(shape, scale):
        return jax.random.normal(next(ks), shape, f32) * scale

    def gain(shape):
        return 1.0 + normal(shape, 0.1)

    n_a, n_b, n_c, n_d = [len(range(m, DEPTH, N_MIXERS)) for m in range(N_MIXERS)]
    ret_in_w = 2 * RET_HEADS * RET_DK + 2 * RET_HEADS * RET_DV
    att_in_w = (ATT_HEADS + 2 * ATT_KV_HEADS) * ATT_HD
    gdn_conv_ch = 2 * GDN_K_HEADS * GDN_DK + GDN_V_HEADS * GDN_DV
    gdn_in_w = gdn_conv_ch + GDN_V_HEADS * GDN_DV + 4 * GDN_V_HEADS
    dt = jnp.exp(jax.random.uniform(next(ks), (n_c, 2, GDN_V_HEADS), f32,
                                    math.log(1e-3), math.log(1e-1)))
    a_init = jax.random.uniform(next(ks), (n_c, 2, GDN_V_HEADS), f32, 1.0, 16.0)
    return {
        'x': normal((BATCH, SEQ, D), 1.0),
        'c': normal((BATCH, D), 1.0),
        'ctx': normal((BATCH, CTX_LEN, D), 1.0),
        'c_ctx': normal((D,), 1.0),
        'mod_w': normal((DEPTH, D, 6 * D), 0.5 * D ** -0.5),
        'mod_b': normal((DEPTH, 6 * D), 0.02),
        'norm1_g': gain((DEPTH, D)),
        'norm2_g': gain((DEPTH, D)),
        'mlp_up': normal((DEPTH, D, MLP_HIDDEN), D ** -0.5),
        'mlp_down': normal((DEPTH, MLP_HIDDEN, D), MLP_HIDDEN ** -0.5),
        'final_g': gain((D,)),
        'ret_w_in': normal((n_a, D, ret_in_w), D ** -0.5),
        'ret_gn_g': gain((n_a, RET_HEADS * RET_DV)),
        'ret_w_out': normal((n_a, RET_HEADS * RET_DV, D), (RET_HEADS * RET_DV) ** -0.5),
        'att_w_in': normal((n_b, D, att_in_w), D ** -0.5),
        'att_sink': normal((n_b, ATT_HEADS), 1.0),
        'att_w_out': normal((n_b, ATT_HEADS * ATT_HD, D), (ATT_HEADS * ATT_HD) ** -0.5),
        'gdn_w_in': normal((n_c, D, gdn_in_w), D ** -0.5),
        'gdn_conv_w': normal((n_c, GDN_CONV, gdn_conv_ch), GDN_CONV ** -0.5),
        'gdn_a_log': jnp.log(a_init),
        'gdn_dt_bias': dt + jnp.log(-jnp.expm1(-dt)),
        'gdn_norm_g': gain((n_c, GDN_DV)),
        'gdn_w_out': normal((n_c, GDN_V_HEADS * GDN_DV, D), (GDN_V_HEADS * GDN_DV) ** -0.5),
        'sgu_w_in': normal((n_d, D, 2 * SGU_WIDTH), D ** -0.5),
        'sgu_ln_g': gain((n_d, SGU_WIDTH)),
        'sgu_ln_b': normal((n_d, SGU_WIDTH), 0.02),
        'sgu_w_s': normal((n_d, SGU_GROUPS, SGU_CHUNK, SGU_CHUNK), SGU_CHUNK ** -0.5),
        'sgu_b_s': gain((n_d, SGU_GROUPS, SGU_CHUNK)),
        'sgu_w_out': normal((n_d, SGU_WIDTH, D), SGU_WIDTH ** -0.5),
    }


def reference(x, c, ctx, c_ctx, mod_w, mod_b, norm1_g, norm2_g, mlp_up, mlp_down, final_g,
              ret_w_in, ret_gn_g, ret_w_out, att_w_in, att_sink, att_w_out,
              gdn_w_in, gdn_conv_w, gdn_a_log, gdn_dt_bias, gdn_norm_g, gdn_w_out,
              sgu_w_in, sgu_ln_g, sgu_ln_b, sgu_w_s, sgu_b_s, sgu_w_out):
    ROWS = x.shape[1] // GRID_W
    cos_r, sin_r = axial_rope_tables(ROWS, RET_DK)
    cos_a, sin_a = axial_rope_tables(ROWS, ATT_HD)
    for i in range(DEPTH):
        kind = i % N_MIXERS
        j = i // N_MIXERS
        want_ctx = ctx_feeds_later(i)
        reads_ctx = kind != MIXER_SGU or want_ctx
        sh1, sc1, g1, sh2, sc2, g2 = ada_params(c, mod_w[i], mod_b[i])
        h_lat = modulate(rms_norm(x, norm1_g[i]), sh1, sc1)
        h_ctx = None
        if reads_ctx:
            csh1, csc1, cg1, csh2, csc2, cg2 = ada_params(c_ctx[None, :], mod_w[i], mod_b[i])
            h_ctx = modulate(rms_norm(ctx, norm1_g[i]), csh1, csc1)
        if kind == MIXER_RET:
            o_lat, o_ctx = retention_mixer(h_lat, h_ctx, ret_w_in[j], ret_gn_g[j], ret_w_out[j],
                                           cos_r, sin_r, want_ctx)
        elif kind == MIXER_WIN:
            o_lat, o_ctx = window_attention_mixer(h_lat, h_ctx, att_w_in[j], att_sink[j], att_w_out[j],
                                                  cos_a, sin_a, want_ctx)
        elif kind == MIXER_GDN:
            o_lat, o_ctx = gdn_mixer(h_lat, h_ctx, gdn_w_in[j], gdn_conv_w[j], gdn_a_log[j],
                                     gdn_dt_bias[j], gdn_norm_g[j], gdn_w_out[j], want_ctx)
        else:
            o_lat = sgu_mixer(h_lat, sgu_w_in[j], sgu_ln_g[j], sgu_ln_b[j], sgu_w_s[j], sgu_b_s[j],
                              sgu_w_out[j])
            o_ctx = None
            if want_ctx:
                o_ctx = sgu_mixer(h_ctx, sgu_w_in[j], sgu_ln_g[j], sgu_ln_b[j], sgu_w_s[j],
                                  sgu_b_s[j], sgu_w_out[j])
        x = x + g1 * o_lat
        x = x + g2 * sq_relu_mlp(modulate(rms_norm(x, norm2_g[i]), sh2, sc2), mlp_up[i], mlp_down[i])
        if want_ctx:
            ctx = ctx + cg1 * o_ctx
            ctx = ctx + cg2 * sq_relu_mlp(modulate(rms_norm(ctx, norm2_g[i]), csh2, csc2),
                                          mlp_up[i], mlp_down[i])
    return rms_norm(x, final_g)
```

```python
import functools
import math

import numpy as np
import jax
import jax.numpy as jnp
from jax import lax
from jax.experimental import pallas as pl
from jax.experimental.pallas import tpu as pltpu

F32 = jnp.float32
BF16 = jnp.bfloat16

D_MODEL = 2048
SEQ = 2048
CTX_LEN = 256
GRID_W = 64
DEPTH = 4
NORM_EPS = 1e-6
ROPE_BASE = 10000.0
MLP_HIDDEN = 4 * D_MODEL

RET_HEADS = 8
RET_DK = 256
RET_DV = 512
RET_CHUNK = 128

ATT_HEADS = 16
ATT_KV_HEADS = 4
ATT_HD = 128
ATT_GROUP = 4
WINDOW = 128
ATT_BLOCK = 128

GDN_K_HEADS = 16
GDN_V_HEADS = 32
GDN_DK = 128
GDN_DV = 128
GDN_CONV = 4
GDN_CHUNK = 64

SGU_CHUNK = 128
SGU_GROUPS = 8
SGU_WIDTH = 2 * D_MODEL

VMEM_LIMIT_BYTES = 56 * 1024 * 1024
MOD_ROWS = 16
ROW_TILE = 512
NEG_BIG = -1e30


def _cparams(*sem):
    return pltpu.CompilerParams(dimension_semantics=sem, vmem_limit_bytes=VMEM_LIMIT_BYTES)


def _silu(x):
    return x * jax.nn.sigmoid(x)


def _identity(x):
    return x


def _gelu_erf(x):
    return 0.5 * x * (1.0 + lax.erf(x * (2.0 ** -0.5)))


def _dot(a, b):
    return jnp.dot(a, b, preferred_element_type=F32)


def _dot3(a, b):
    a_hi = a.astype(BF16)
    b_hi = b.astype(BF16)
    a_lo = (a - a_hi.astype(F32)).astype(BF16)
    b_lo = (b - b_hi.astype(F32)).astype(BF16)
    return _dot(a_hi, b_hi) + (_dot(a_hi, b_lo) + _dot(a_lo, b_hi))


def _dot_nt(a, b):
    return lax.dot_general(a, b, (((1,), (1,)), ((), ())), preferred_element_type=F32)


def _dot_tn(a, b):
    return lax.dot_general(a, b, (((0,), (0,)), ((), ())), preferred_element_type=F32)


def _mods_kernel(cond_ref, w_ref, b_ref, o_ref):
    a = _silu(cond_ref[...]).astype(BF16)
    o_ref[...] = _dot(a, w_ref[...].astype(BF16)) + b_ref[...]


def ada_mods(cond, mod_w, mod_b):
    depth, d, n = mod_w.shape
    tn = 1024
    return pl.pallas_call(
        _mods_kernel,
        grid=(depth, n // tn),
        in_specs=[
            pl.BlockSpec((MOD_ROWS, d), lambda l, j: (0, 0)),
            pl.BlockSpec((None, d, tn), lambda l, j: (l, 0, j)),
            pl.BlockSpec((None, 1, tn), lambda l, j: (l, 0, j)),
        ],
        out_specs=pl.BlockSpec((None, MOD_ROWS, tn), lambda l, j: (l, 0, j)),
        out_shape=jax.ShapeDtypeStruct((depth, MOD_ROWS, n), F32),
        compiler_params=_cparams("arbitrary", "arbitrary"),
        name="ada_mods",
    )(cond, mod_w, mod_b.reshape(depth, 1, n))


def _mod_spec(row_fn, chunk, tn=None):
    if tn is None:
        return pl.BlockSpec((None, 1, D_MODEL), lambda i, j: (row_fn(i), 0, chunk))
    per = D_MODEL // tn
    return pl.BlockSpec((None, 1, tn), lambda i, j: (row_fn(i), 0, chunk * per + j))


def _row_fn(rows_per_batch, tm, is_ctx, batch):
    if is_ctx:
        return lambda i: batch
    per = rows_per_batch // tm
    return lambda i: i // per


def _norm_mod(x, g, shift, scale):
    ms = jnp.mean(x * x, axis=-1, keepdims=True)
    y = x * lax.rsqrt(ms + NORM_EPS) * g
    return y * (1.0 + scale) + shift


def _linear_pre_kernel(x_ref, g_ref, sh_ref, sc_ref, w_ref, *rest, segs):
    out_refs = rest[:len(segs)]
    h_ref = rest[len(segs)]
    j = pl.program_id(1)

    @pl.when(j == 0)
    def _():
        h_ref[...] = _norm_mod(x_ref[...], g_ref[...], sh_ref[...], sc_ref[...]).astype(BF16)

    acc = _dot(h_ref[...], w_ref[...])
    start = 0
    for (ntiles, _, act), o_ref in zip(segs, out_refs):
        @pl.when((j >= start) & (j < start + ntiles))
        def _(o_ref=o_ref, act=act):
            o_ref[...] = act(acc).astype(o_ref.dtype)
        start += ntiles


def linear_pre(x, norm_g, mods, row_fn, w, segments, tn, tm=ROW_TILE, name="linear_pre"):
    m, d = x.shape
    n = w.shape[1]
    tm = min(tm, m)
    segs = []
    out_specs = []
    out_shapes = []
    start = 0
    for width, dtype, act in segments:
        nt = width // tn
        assert nt * tn == width
        segs.append((nt, dtype, act))
        out_specs.append(pl.BlockSpec(
            (tm, tn), lambda i, j, s=start, nt=nt: (i, jnp.clip(j - s, 0, nt - 1))))
        out_shapes.append(jax.ShapeDtypeStruct((m, width), dtype))
        start += nt
    assert start * tn == n
    outs = pl.pallas_call(
        functools.partial(_linear_pre_kernel, segs=tuple(segs)),
        grid=(m // tm, n // tn),
        in_specs=[
            pl.BlockSpec((tm, d), lambda i, j: (i, 0)),
            pl.BlockSpec((1, d), lambda i, j: (0, 0)),
            _mod_spec(row_fn, 0),
            _mod_spec(row_fn, 1),
            pl.BlockSpec((d, tn), lambda i, j: (0, j)),
        ],
        out_specs=out_specs,
        out_shape=out_shapes,
        scratch_shapes=[pltpu.VMEM((tm, d), BF16)],
        compiler_params=_cparams("arbitrary", "arbitrary"),
        name=name,
    )(x, norm_g.reshape(1, d), mods, mods, w)
    return outs


def _linear_post_kernel(a_ref, w_ref, res_ref, gate_ref, o_ref):
    o_ref[...] = res_ref[...] + gate_ref[...] * _dot(a_ref[...], w_ref[...])


def linear_post(a, w, res, mods, row_fn, gate_chunk, tn=1024, tm=ROW_TILE, name="linear_post"):
    m, k = a.shape
    n = w.shape[1]
    tm = min(tm, m)
    return pl.pallas_call(
        _linear_post_kernel,
        grid=(m // tm, n // tn),
        in_specs=[
            pl.BlockSpec((tm, k), lambda i, j: (i, 0)),
            pl.BlockSpec((k, tn), lambda i, j: (0, j)),
            pl.BlockSpec((tm, tn), lambda i, j: (i, j)),
            _mod_spec(row_fn, gate_chunk, tn),
        ],
        out_specs=pl.BlockSpec((tm, tn), lambda i, j: (i, j)),
        out_shape=jax.ShapeDtypeStruct((m, n), F32),
        compiler_params=_cparams("arbitrary", "arbitrary"),
        name=name,
    )(a, w, res, mods)


def _mlp_kernel(x_ref, g_ref, sh_ref, sc_ref, gate_ref, wu_ref, wd_ref, fg_ref, o_ref,
                h_ref, acc_ref, *, final_norm):
    j = pl.program_id(1)

    @pl.when(j == 0)
    def _():
        h_ref[...] = _norm_mod(x_ref[...], g_ref[...], sh_ref[...], sc_ref[...]).astype(BF16)
        acc_ref[...] = jnp.zeros_like(acc_ref)

    a = jnp.maximum(_dot(h_ref[...], wu_ref[...]), 0.0)
    acc_ref[...] += _dot((a * a).astype(BF16), wd_ref[...])

    @pl.when(j == pl.num_programs(1) - 1)
    def _():
        y = x_ref[...] + gate_ref[...] * acc_ref[...]
        if final_norm:
            ms = jnp.mean(y * y, axis=-1, keepdims=True)
            y = y * lax.rsqrt(ms + NORM_EPS) * fg_ref[...]
        o_ref[...] = y


def mlp_block(x, norm_g, mods, row_fn, w_up, w_down, final_g=None, th=512, tm=ROW_TILE, name="mlp"):
    m, d = x.shape
    hid = w_up.shape[1]
    tm = min(tm, m)
    final_norm = final_g is not None
    fg = (final_g if final_norm else norm_g).reshape(1, d)
    return pl.pallas_call(
        functools.partial(_mlp_kernel, final_norm=final_norm),
        grid=(m // tm, hid // th),
        in_specs=[
            pl.BlockSpec((tm, d), lambda i, j: (i, 0)),
            pl.BlockSpec((1, d), lambda i, j: (0, 0)),
            _mod_spec(row_fn, 3),
            _mod_spec(row_fn, 4),
            _mod_spec(row_fn, 5),
            pl.BlockSpec((d, th), lambda i, j: (0, j)),
            pl.BlockSpec((th, d), lambda i, j: (j, 0)),
            pl.BlockSpec((1, d), lambda i, j: (0, 0)),
        ],
        out_specs=pl.BlockSpec((tm, d), lambda i, j: (i, 0)),
        out_shape=jax.ShapeDtypeStruct((m, d), F32),
        scratch_shapes=[pltpu.VMEM((tm, d), BF16), pltpu.VMEM((tm, d), F32)],
        compiler_params=_cparams("arbitrary", "arbitrary"),
        name=name,
    )(x, norm_g.reshape(1, d), mods, mods, mods, w_up, w_down, fg)


def rope_tables(head_dim):
    rows = SEQ // GRID_W
    row = jnp.repeat(jnp.arange(rows, dtype=F32), GRID_W)
    col = jnp.tile(jnp.arange(GRID_W, dtype=F32), rows)
    axis_dim = head_dim // 2
    inv_freq = jnp.exp(-math.log(ROPE_BASE) * jnp.arange(0, axis_dim, 2, dtype=F32) / axis_dim)
    ang = jnp.concatenate([row[:, None] * inv_freq, col[:, None] * inv_freq], axis=-1)
    cos = jnp.repeat(jnp.cos(ang), 2, axis=-1)
    sin = jnp.repeat(jnp.sin(ang), 2, axis=-1)
    sign = jnp.where(jnp.arange(head_dim) % 2 == 0, -1.0, 1.0).astype(F32)
    return cos, sin * sign


def _rope(x, cos, sin_signed):
    width = x.shape[-1]
    nxt = pltpu.roll(x, width - 1, 1)
    prv = pltpu.roll(x, 1, 1)
    lane = lax.broadcasted_iota(jnp.int32, x.shape, 1)
    partner = jnp.where(lane % 2 == 0, nxt, prv)
    return x * cos + partner * sin_signed


def _ret_tables():
    lg = np.log1p(-np.exp2(-5.0 - np.arange(RET_HEADS, dtype=np.float64)))[:, None]
    pos = np.arange(RET_CHUNK, dtype=np.float64)
    dist = np.abs(pos[:, None] - pos[None, :])
    intra = np.exp(lg[:, :, None] * dist) * (1.0 + np.eye(RET_CHUNK))
    q_f = np.exp(lg * (pos + 1.0))
    q_b = np.exp(lg * (RET_CHUNK - pos))
    k_f = np.exp(lg * (RET_CHUNK - 1.0 - pos))
    k_b = np.exp(lg * pos)
    dec = np.stack([q_f, q_b, k_f, k_b], axis=1)
    dec = np.broadcast_to(dec[..., None], dec.shape + (RET_DK,))
    cdec = np.broadcast_to(np.exp(lg * RET_CHUNK)[:, :, None], (RET_HEADS, 8, RET_DV))
    return (jnp.asarray(intra, F32), jnp.asarray(dec, F32), jnp.asarray(cdec, F32))


def _ret_kernel(ql_ref, kl_ref, vl_ref, gl_ref, qc_ref, kc_ref, vc_ref, gc_ref,
                cos_ref, sin_ref, intra_ref, dec_ref, cdec_ref, gn_ref,
                ol_ref, oc_ref, qs_ref, ks_ref, sb_ref, st_ref):
    C = RET_CHUNK
    n_lat = SEQ // C
    n_ctx = CTX_LEN // C
    kscale = RET_DK ** -0.5
    cd = cdec_ref[0:1, :]

    def rope_chunk(c, carry):
        rows = pl.ds(pl.multiple_of(c * C, C), C)
        cs = cos_ref[rows, :]
        sn = sin_ref[rows, :]
        qs_ref[rows, :] = _rope(ql_ref[rows, :], cs, sn)
        ks_ref[rows, :] = _rope(kl_ref[rows, :], cs, sn) * kscale
        return carry

    lax.fori_loop(0, n_lat, rope_chunk, 0)

    st_ref[...] = jnp.zeros_like(st_ref)

    def back_step(k, v, slot):
        sb_ref[slot] = st_ref[...].astype(BF16)
        kd = (k * dec_ref[3]).astype(BF16)
        st_ref[...] = st_ref[...] * cd + _dot_tn(kd, v)

    for c in reversed(range(n_ctx)):
        rows = pl.ds(c * C, C)
        back_step(kc_ref[rows, :] * kscale, vc_ref[rows, :], c)

    def back_lat(t, carry):
        c = n_lat - 1 - t
        rows = pl.ds(pl.multiple_of(c * C, C), C)
        back_step(ks_ref[rows, :], vl_ref[rows, :], n_ctx + c)
        return carry

    lax.fori_loop(0, n_lat, back_lat, 0)

    st_ref[...] = jnp.zeros_like(st_ref)

    def fwd_step(q, k, v, gate, slot):
        s = _dot_nt(q.astype(BF16), k.astype(BF16)) * intra_ref[...]
        st = st_ref[...]
        o = (_dot(s.astype(BF16), v)
             + _dot((q * dec_ref[0]).astype(BF16), st.astype(BF16))
             + _dot((q * dec_ref[1]).astype(BF16), sb_ref[slot]))
        st_ref[...] = st * cd + _dot_tn((k * dec_ref[2]).astype(BF16), v)
        ms = jnp.mean(o * o, axis=-1, keepdims=True)
        on = o * lax.rsqrt(ms + NORM_EPS) * gn_ref[...]
        return (on * gate.astype(F32)).astype(BF16)

    for c in range(n_ctx):
        rows = pl.ds(c * C, C)
        oc_ref[rows, :] = fwd_step(qc_ref[rows, :], kc_ref[rows, :] * kscale, vc_ref[rows, :],
                                   gc_ref[rows, :], c)

    def fwd_lat(c, carry):
        rows = pl.ds(pl.multiple_of(c * C, C), C)
        ol_ref[rows, :] = fwd_step(qs_ref[rows, :], ks_ref[rows, :], vl_ref[rows, :],
                                   gl_ref[rows, :], n_ctx + c)
        return carry

    lax.fori_loop(0, n_lat, fwd_lat, 0)


def retention_core(qk_lat, v_lat, g_lat, qk_ctx, v_ctx, g_ctx, gn_g, batch):
    H = RET_HEADS
    cos, sin = rope_tables(RET_DK)
    intra, dec, cdec = _ret_tables()
    n_chunks = (SEQ + CTX_LEN) // RET_CHUNK

    def specs(rows):
        return [
            pl.BlockSpec((rows, RET_DK), lambda b, h: (b, h)),
            pl.BlockSpec((rows, RET_DK), lambda b, h: (b, H + h)),
            pl.BlockSpec((rows, RET_DV), lambda b, h: (b, h)),
            pl.BlockSpec((rows, RET_DV), lambda b, h: (b, h)),
        ]

    return pl.pallas_call(
        _ret_kernel,
        grid=(batch, H),
        in_specs=specs(SEQ) + specs(CTX_LEN) + [
            pl.BlockSpec((SEQ, RET_DK), lambda b, h: (0, 0)),
            pl.BlockSpec((SEQ, RET_DK), lambda b, h: (0, 0)),
            pl.BlockSpec((None, RET_CHUNK, RET_CHUNK), lambda b, h: (h, 0, 0)),
            pl.BlockSpec((None, 4, RET_CHUNK, RET_DK), lambda b, h: (h, 0, 0, 0)),
            pl.BlockSpec((None, 8, RET_DV), lambda b, h: (h, 0, 0)),
            pl.BlockSpec((1, RET_DV), lambda b, h: (0, h)),
        ],
        out_specs=[
            pl.BlockSpec((SEQ, RET_DV), lambda b, h: (b, h)),
            pl.BlockSpec((CTX_LEN, RET_DV), lambda b, h: (b, h)),
        ],
        out_shape=[
            jax.ShapeDtypeStruct((batch * SEQ, H * RET_DV), BF16),
            jax.ShapeDtypeStruct((batch * CTX_LEN, H * RET_DV), BF16),
        ],
        scratch_shapes=[
            pltpu.VMEM((SEQ, RET_DK), F32),
            pltpu.VMEM((SEQ, RET_DK), F32),
            pltpu.VMEM((n_chunks, RET_DK, RET_DV), BF16),
            pltpu.VMEM((RET_DK, RET_DV), F32),
        ],
        compiler_params=_cparams("arbitrary", "arbitrary"),
        name="retention_core",
    )(qk_lat, qk_lat, v_lat, g_lat, qk_ctx, qk_ctx, v_ctx, g_ctx,
      cos, sin, intra, dec, cdec, gn_g.reshape(1, H * RET_DV))


def _att_kernel(sink_ref, ql_ref, kl_ref, vl_ref, qc_ref, kc_ref, vc_ref, cos_ref, sin_ref,
                ol_ref, oc_ref, kp_ref, vp_ref):
    BLK = ATT_BLOCK
    G = ATT_GROUP
    HD = ATT_HD
    n_blk = SEQ // BLK
    span = BLK + 2 * WINDOW
    scale = HD ** -0.5
    kvh = pl.program_id(1)

    zpad = jnp.zeros((WINDOW, HD), BF16)
    kp_ref[pl.ds(0, WINDOW), :] = zpad
    vp_ref[pl.ds(0, WINDOW), :] = zpad
    kp_ref[pl.ds(WINDOW + SEQ, WINDOW), :] = zpad
    vp_ref[pl.ds(WINDOW + SEQ, WINDOW), :] = zpad
    ctx0 = SEQ + 2 * WINDOW
    kp_ref[pl.ds(ctx0, CTX_LEN), :] = kc_ref[...].astype(BF16)
    vp_ref[pl.ds(ctx0, CTX_LEN), :] = vc_ref[...]

    def stage(c, carry):
        rows = pl.ds(pl.multiple_of(c * BLK, BLK), BLK)
        dst = pl.ds(pl.multiple_of(c * BLK + WINDOW, BLK), BLK)
        kp_ref[dst, :] = _rope(kl_ref[rows, :], cos_ref[rows, :], sin_ref[rows, :]).astype(BF16)
        vp_ref[dst, :] = vl_ref[rows, :]
        return carry

    lax.fori_loop(0, n_blk, stage, 0)

    ridx = lax.broadcasted_iota(jnp.int32, (G * BLK, 1), 0)
    grp = ridx // BLK
    sink_col = jnp.zeros((G * BLK, 1), F32)
    for g in range(G):
        sink_col = jnp.where(grp == g, sink_ref[kvh * G + g], sink_col)
    kc_b = kp_ref[pl.ds(ctx0, CTX_LEN), :]
    vc_b = vp_ref[pl.ds(ctx0, CTX_LEN), :]

    def finish(parts, out_ref, rows):
        m = sink_col
        for s, _ in parts:
            m = jnp.maximum(m, jnp.max(s, axis=-1, keepdims=True))
        den = jnp.exp(sink_col - m)
        acc = None
        for s, v in parts:
            p = jnp.exp(s - m)
            den = den + jnp.sum(p, axis=-1, keepdims=True)
            pv = _dot(p.astype(BF16), v)
            acc = pv if acc is None else acc + pv
        o = acc / den
        for g in range(G):
            out_ref[rows, pl.ds(g * HD, HD)] = o[g * BLK:(g + 1) * BLK, :].astype(BF16)

    qi = ridx % BLK
    col = lax.broadcasted_iota(jnp.int32, (G * BLK, span), 1)
    band = (col >= qi) & (col <= qi + 2 * WINDOW)

    def lat_block(bi, carry):
        rows = pl.ds(pl.multiple_of(bi * BLK, BLK), BLK)
        cs = cos_ref[rows, :]
        sn = sin_ref[rows, :]
        qs = [(_rope(ql_ref[rows, pl.ds(g * HD, HD)], cs, sn) * scale).astype(BF16) for g in range(G)]
        q4 = jnp.concatenate(qs, axis=0)
        krows = pl.ds(pl.multiple_of(bi * BLK, BLK), span)
        s_lat = _dot_nt(q4, kp_ref[krows, :])
        kpos = col + (bi * BLK - WINDOW)
        valid = band & (kpos >= 0) & (kpos < SEQ)
        s_lat = jnp.where(valid, s_lat, NEG_BIG)
        s_ctx = _dot_nt(q4, kc_b)
        finish([(s_lat, vp_ref[krows, :]), (s_ctx, vc_b)], ol_ref, rows)
        return carry

    lax.fori_loop(0, n_blk, lat_block, 0)

    for bi in range(CTX_LEN // BLK):
        rows = pl.ds(bi * BLK, BLK)
        qs = [(qc_ref[rows, pl.ds(g * HD, HD)] * scale).astype(BF16) for g in range(G)]
        q4 = jnp.concatenate(qs, axis=0)
        finish([(_dot_nt(q4, kc_b), vc_b)], oc_ref, rows)


def attention_core(q_lat, k_lat, v_lat, q_ctx, k_ctx, v_ctx, sink, batch):
    cos, sin = rope_tables(ATT_HD)
    GW = ATT_GROUP * ATT_HD
    pad_rows = SEQ + 2 * WINDOW + CTX_LEN

    def specs(rows):
        return [
            pl.BlockSpec((rows, GW), lambda b, h: (b, h)),
            pl.BlockSpec((rows, ATT_HD), lambda b, h: (b, h)),
            pl.BlockSpec((rows, ATT_HD), lambda b, h: (b, h)),
        ]

    return pl.pallas_call(
        _att_kernel,
        grid=(batch, ATT_KV_HEADS),
        in_specs=[pl.BlockSpec(memory_space=pltpu.SMEM)] + specs(SEQ) + specs(CTX_LEN) + [
            pl.BlockSpec((SEQ, ATT_HD), lambda b, h: (0, 0)),
            pl.BlockSpec((SEQ, ATT_HD), lambda b, h: (0, 0)),
        ],
        out_specs=[
            pl.BlockSpec((SEQ, GW), lambda b, h: (b, h)),
            pl.BlockSpec((CTX_LEN, GW), lambda b, h: (b, h)),
        ],
        out_shape=[
            jax.ShapeDtypeStruct((batch * SEQ, ATT_HEADS * ATT_HD), BF16),
            jax.ShapeDtypeStruct((batch * CTX_LEN, ATT_HEADS * ATT_HD), BF16),
        ],
        scratch_shapes=[
            pltpu.VMEM((pad_rows, ATT_HD), BF16),
            pltpu.VMEM((pad_rows, ATT_HD), BF16),
        ],
        compiler_params=_cparams("arbitrary", "arbitrary"),
        name="attention_core",
    )(sink.astype(F32), q_lat, k_lat, v_lat, q_ctx, k_ctx, v_ctx, cos, sin)


def _sgu_kernel(u_ref, v_ref, g_ref, b_ref, ws_ref, bs_ref, o_ref, *, chunks):
    C = SGU_CHUNK
    gw = SGU_WIDTH // SGU_GROUPS
    for c in range(chunks):
        rows = pl.ds(c * C, C)
        v = v_ref[rows, :]
        mu = jnp.mean(v, axis=-1, keepdims=True)
        xc = v - mu
        var = jnp.mean(xc * xc, axis=-1, keepdims=True)
        vn = (xc * lax.rsqrt(var + NORM_EPS) * g_ref[...] + b_ref[...]).astype(BF16)
        for g in range(SGU_GROUPS):
            cols = pl.ds(g * gw, gw)
            t = _dot(ws_ref[g], vn[:, g * gw:(g + 1) * gw]) + bs_ref[g]
            o_ref[rows, cols] = (u_ref[rows, cols].astype(F32) * t).astype(BF16)


def sgu_core(u, v, ln_g, ln_b, w_s, b_s, chunks=2):
    m = u.shape[0]
    tm = chunks * SGU_CHUNK
    return pl.pallas_call(
        functools.partial(_sgu_kernel, chunks=chunks),
        grid=(m // tm,),
        in_specs=[
            pl.BlockSpec((tm, SGU_WIDTH), lambda i: (i, 0)),
            pl.BlockSpec((tm, SGU_WIDTH), lambda i: (i, 0)),
            pl.BlockSpec((1, SGU_WIDTH), lambda i: (0, 0)),
            pl.BlockSpec((1, SGU_WIDTH), lambda i: (0, 0)),
            pl.BlockSpec((SGU_GROUPS, SGU_CHUNK, SGU_CHUNK), lambda i: (0, 0, 0)),
            pl.BlockSpec((SGU_GROUPS, SGU_CHUNK, 1), lambda i: (0, 0, 0)),
        ],
        out_specs=pl.BlockSpec((tm, SGU_WIDTH), lambda i: (i, 0)),
        out_shape=jax.ShapeDtypeStruct((m, SGU_WIDTH), BF16),
        compiler_params=_cparams("arbitrary"),
        name="sgu_core",
    )(u, v, ln_g.reshape(1, -1), ln_b.reshape(1, -1), w_s.astype(BF16),
      b_s.reshape(SGU_GROUPS, SGU_CHUNK, 1))


GDN_ROWS = CTX_LEN + SEQ
GDN_NCHUNK = GDN_ROWS // GDN_CHUNK
GDN_CONV_ROWS = 256
GDN_HALO = 8


def _gdn_kernel(ql_ref, kl_ref, vl_ref, zl_ref, qc_ref, kc_ref, vc_ref, ba_ref,
                cwq_ref, cwk_ref, cwv_ref, alog_ref, dt_ref, ng_ref,
                ol_ref,
                xp_ref, qn_ref, kn_ref, vv_ref, gt_ref, u_ref, w_ref, a_ref, qg_ref, kg_ref,
                el_ref, of_ref, ob_ref, s_ref):
    C = GDN_CHUNK
    DK = GDN_DK
    R = GDN_CONV_ROWS
    HALO = GDN_HALO

    def conv_silu(src_ref, n_rows, cw_ref, width, store):
        cols = pl.ds(0, width)
        xp_ref[pl.ds(0, HALO), cols] = jnp.zeros((HALO, width), F32)
        xp_ref[pl.ds(HALO, n_rows), cols] = src_ref[...]
        xp_ref[pl.ds(HALO + n_rows, HALO), cols] = jnp.zeros((HALO, width), F32)
        w = cw_ref[...]
        n = R + 2 * HALO

        def body(c, carry):
            r0 = pl.multiple_of(c * R, R)
            x = xp_ref[pl.ds(r0, n), cols]
            y = (w[2:3] * x + w[1:2] * pltpu.roll(x, 1, 0) + w[0:1] * pltpu.roll(x, 2, 0)
                 + w[3:4] * pltpu.roll(x, n - 1, 0))[HALO:HALO + R]
            store(r0, _silu(y))
            return carry

        lax.fori_loop(0, n_rows // R, body, 0)

    def unit(y):
        return y * lax.rsqrt(jnp.sum(y * y, axis=-1, keepdims=True) + 1e-6)

    def store_q(off):
        def f(r0, y):
            qn_ref[pl.ds(pl.multiple_of(off + r0, R), R), :] = unit(y) * (DK ** -0.5)
        return f

    def store_k(off):
        def f(r0, y):
            kn_ref[pl.ds(pl.multiple_of(off + r0, R), R), :] = unit(y)
        return f

    def store_v(off):
        def f(r0, y):
            vv_ref[pl.ds(pl.multiple_of(off + r0, R), R), :] = y
        return f

    conv_silu(qc_ref, CTX_LEN, cwq_ref, DK, store_q(0))
    conv_silu(kc_ref, CTX_LEN, cwk_ref, DK, store_k(0))
    conv_silu(vc_ref, CTX_LEN, cwv_ref, 2 * GDN_DV, store_v(0))
    conv_silu(ql_ref, SEQ, cwq_ref, DK, store_q(CTX_LEN))
    conv_silu(kl_ref, SEQ, cwk_ref, DK, store_k(CTX_LEN))
    conv_silu(vl_ref, SEQ, cwv_ref, 2 * GDN_DV, store_v(CTX_LEN))

    raw = ba_ref[...]
    xg = raw + dt_ref[...]
    softplus = jnp.maximum(xg, 0.0) + jnp.log1p(jnp.exp(-jnp.abs(xg)))
    g = -jnp.exp(alog_ref[...]) * softplus
    pos = lax.broadcasted_iota(jnp.int32, raw.shape, 1) % C
    pre = g
    suf = g
    shift = 1
    while shift < C:
        pre = pre + jnp.where(pos >= shift, pltpu.roll(pre, shift, 1), 0.0)
        suf = suf + jnp.where(pos + shift < C, pltpu.roll(suf, GDN_ROWS - shift, 1), 0.0)
        shift *= 2
    row = lax.broadcasted_iota(jnp.int32, raw.shape, 0)
    gt_ref[...] = jnp.where(row < 4, jax.nn.sigmoid(raw), jnp.where(row < 6, pre, suf))

    ri = lax.broadcasted_iota(jnp.int32, (C, C), 0)
    ci = lax.broadcasted_iota(jnp.int32, (C, C), 1)
    eye = (ri == ci).astype(F32)
    incl = (ri >= ci, ri <= ci)
    strict = (ri > ci, ri < ci)

    def to_col(rowvec):
        return jnp.sum(eye * rowvec, axis=1, keepdims=True)

    def prep_pair(p, carry):
        lanes = pl.ds(pl.multiple_of(p * 2 * C, 2 * C), 2 * C)
        gates = gt_ref[:, lanes]
        for half in range(2):
            n = p * 2 + half
            rows = pl.ds(pl.multiple_of(n * C, C), C)
            q = qn_ref[rows, :]
            k = kn_ref[rows, :]
            k16 = k.astype(BF16)
            qk_kk = _dot_nt(jnp.concatenate([q, k], axis=0).astype(BF16), k16)
            qk = qk_kk[:C]
            kk = qk_kk[C:]
            for d in range(2):
                for j in range(2):
                    combo = 2 * d + j
                    brow = gates[combo:combo + 1, half * C:(half + 1) * C]
                    grow = gates[4 + combo:5 + combo, half * C:(half + 1) * C]
                    bcol = to_col(brow)
                    gcol = to_col(grow)
                    dec = jnp.exp(jnp.where(incl[d], gcol - grow, NEG_BIG))
                    nmat = jnp.where(strict[d], -(kk * bcol * dec), 0.0)
                    pm = eye + nmat
                    mm = _dot3(nmat, nmat)
                    for _ in range(4):
                        r = _dot3(jnp.concatenate([pm, mm], axis=0), mm)
                        pm = pm + r[:C]
                        mm = r[C:]
                    tmat = pm + _dot3(pm, mm)
                    v = vv_ref[rows, pl.ds(j * GDN_DV, GDN_DV)]
                    eg = jnp.exp(gcol)
                    rhs = jnp.concatenate([v * bcol, k * (bcol * eg)], axis=1)
                    uw = _dot(tmat.astype(BF16), rhs.astype(BF16))
                    u_ref[combo, rows, :] = uw[:, :GDN_DV]
                    w_ref[combo, rows, :] = uw[:, GDN_DV:].astype(BF16)
                    a_ref[combo, rows, :] = (qk * dec).astype(BF16)
                    g_last = grow[:, C - 1:C] if d == 0 else grow[:, 0:1]
                    qg_ref[combo, rows, :] = (q * eg).astype(BF16)
                    kg_ref[combo, rows, :] = (k * jnp.exp(g_last - gcol)).astype(BF16)
                    el_ref[combo, n] = jnp.broadcast_to(jnp.exp(g_last), (8, GDN_DV))
        return carry

    lax.fori_loop(0, GDN_NCHUNK // 2, prep_pair, 0)

    s_ref[...] = jnp.zeros_like(s_ref)

    def rec_step(combo, n, out_ref):
        j = combo % 2
        rows = pl.ds(pl.multiple_of(n * C, C), C)
        st = s_ref[combo]
        s16 = st.astype(BF16)
        wq = jnp.concatenate([w_ref[combo, rows, :], qg_ref[combo, rows, :]], axis=0)
        r = _dot(wq, s16)
        v_new = (u_ref[combo, rows, :] - r[:C]).astype(BF16)
        out_ref[rows, pl.ds(j * GDN_DV, GDN_DV)] = r[C:] + _dot(a_ref[combo, rows, :], v_new)
        e_last = el_ref[combo, n][0:1, :]
        s_ref[combo] = st * e_last + _dot_tn(kg_ref[combo, rows, :], v_new)

    def rec_pair(nf, nb):
        for j in range(2):
            rec_step(j, nf, of_ref)
            rec_step(2 + j, nb, ob_ref)

    n_ctx = CTX_LEN // C
    for t in range(n_ctx):
        rec_pair(t, n_ctx - 1 - t)

    def rec_lat(t, carry):
        rec_pair(n_ctx + t, GDN_NCHUNK - 1 - t)
        return carry

    lax.fori_loop(0, SEQ // C, rec_lat, 0)

    def fin(c, carry):
        src = pl.ds(pl.multiple_of(CTX_LEN + c * R, R), R)
        dst = pl.ds(pl.multiple_of(c * R, R), R)
        for j in range(2):
            cols = pl.ds(j * GDN_DV, GDN_DV)
            o = of_ref[src, cols] + ob_ref[src, cols]
            ms = jnp.mean(o * o, axis=-1, keepdims=True)
            on = o * lax.rsqrt(ms + NORM_EPS) * ng_ref[...]
            ol_ref[dst, cols] = (on * zl_ref[dst, cols].astype(F32)).astype(BF16)
        return carry

    lax.fori_loop(0, SEQ // R, fin, 0)


def gdn_core(mixed_lat, z_lat, mixed_ctx, ba, conv_w, a_log, dt_bias, norm_g, batch):
    HK = GDN_K_HEADS
    DK = GDN_DK
    VW = 2 * GDN_DV
    v_blk0 = (2 * HK * DK) // VW

    def tab(t):
        t = t.reshape(2, HK, 2).transpose(1, 0, 2).reshape(HK, 4)
        return jnp.concatenate([jnp.zeros_like(t), t], axis=1).reshape(HK, 8, 1).astype(F32)

    def specs(rows):
        return [
            pl.BlockSpec((rows, DK), lambda b, h: (b, h)),
            pl.BlockSpec((rows, DK), lambda b, h: (b, HK + h)),
            pl.BlockSpec((rows, VW), lambda b, h: (b, v_blk0 + h)),
        ]

    nc = GDN_NCHUNK
    return pl.pallas_call(
        _gdn_kernel,
        grid=(batch, HK),
        in_specs=specs(SEQ) + [pl.BlockSpec((SEQ, VW), lambda b, h: (b, h))] + specs(CTX_LEN) + [
            pl.BlockSpec((None, None, 8, GDN_ROWS), lambda b, h: (b, h, 0, 0)),
            pl.BlockSpec((GDN_CONV, DK), lambda b, h: (0, h)),
            pl.BlockSpec((GDN_CONV, DK), lambda b, h: (0, HK + h)),
            pl.BlockSpec((GDN_CONV, VW), lambda b, h: (0, v_blk0 + h)),
            pl.BlockSpec((None, 8, 1), lambda b, h: (h, 0, 0)),
            pl.BlockSpec((None, 8, 1), lambda b, h: (h, 0, 0)),
            pl.BlockSpec((1, GDN_DV), lambda b, h: (0, 0)),
        ],
        out_specs=pl.BlockSpec((SEQ, VW), lambda b, h: (b, h)),
        out_shape=jax.ShapeDtypeStruct((batch * SEQ, GDN_V_HEADS * GDN_DV), BF16),
        scratch_shapes=[
            pltpu.VMEM((SEQ + 2 * GDN_HALO, VW), F32),
            pltpu.VMEM((GDN_ROWS, DK), F32),
            pltpu.VMEM((GDN_ROWS, DK), F32),
            pltpu.VMEM((GDN_ROWS, VW), F32),
            pltpu.VMEM((8, GDN_ROWS), F32),
            pltpu.VMEM((4, GDN_ROWS, GDN_DV), F32),
            pltpu.VMEM((4, GDN_ROWS, DK), BF16),
            pltpu.VMEM((4, GDN_ROWS, GDN_CHUNK), BF16),
            pltpu.VMEM((4, GDN_ROWS, DK), BF16),
            pltpu.VMEM((4, GDN_ROWS, DK), BF16),
            pltpu.VMEM((4, nc, 8, GDN_DV), F32),
            pltpu.VMEM((GDN_ROWS, VW), F32),
            pltpu.VMEM((GDN_ROWS, VW), F32),
            pltpu.VMEM((4, DK, GDN_DV), F32),
        ],
        compiler_params=_cparams("arbitrary", "arbitrary"),
        name="gdn_core",
    )(mixed_lat, mixed_lat, mixed_lat, z_lat, mixed_ctx, mixed_ctx, mixed_ctx, ba,
      conv_w, conv_w, conv_w, tab(a_log), tab(dt_bias), norm_g.reshape(1, GDN_DV))


def _finish_layer(o_lat, o_ctx, xl, xc, w_out, norm2_g, up, down, mods, rows, final_g=None):
    lat_row, ctx_row = rows
    w_out = w_out.astype(BF16)
    up = up.astype(BF16)
    down = down.astype(BF16)
    xl = linear_post(o_lat, w_out, xl, mods, lat_row, 2, name="out_proj_lat")
    xl = mlp_block(xl, norm2_g, mods, lat_row, up, down, final_g=final_g, name="mlp_lat")
    if o_ctx is not None:
        xc = linear_post(o_ctx, w_out, xc, mods, ctx_row, 2, name="out_proj_ctx")
        xc = mlp_block(xc, norm2_g, mods, ctx_row, up, down, name="mlp_ctx")
    return xl, xc


def retention_layer(xl, xc, mods, rows, batch, norm1_g, norm2_g, w_in, gn_g, w_out, up, down):
    lat_row, ctx_row = rows
    w_in = w_in.astype(BF16)
    qkw = 2 * RET_HEADS * RET_DK
    vw = RET_HEADS * RET_DV
    segs = [(qkw, F32, _identity), (vw, BF16, _identity), (vw, BF16, _silu)]
    qk_l, v_l, g_l = linear_pre(xl, norm1_g, mods, lat_row, w_in, segs, 1024, name="ret_in_lat")
    qk_c, v_c, g_c = linear_pre(xc, norm1_g, mods, ctx_row, w_in, segs, 1024, name="ret_in_ctx")
    o_l, o_c = retention_core(qk_l, v_l, g_l, qk_c, v_c, g_c, gn_g, batch)
    return _finish_layer(o_l, o_c, xl, xc, w_out, norm2_g, up, down, mods, rows)


def attention_layer(xl, xc, mods, rows, batch, norm1_g, norm2_g, w_in, sink, w_out, up, down):
    lat_row, ctx_row = rows
    w_in = w_in.astype(BF16)
    qw = ATT_HEADS * ATT_HD
    kvw = ATT_KV_HEADS * ATT_HD
    segs = [(qw, F32, _identity), (kvw, F32, _identity), (kvw, BF16, _identity)]
    q_l, k_l, v_l = linear_pre(xl, norm1_g, mods, lat_row, w_in, segs, 512, name="att_in_lat")
    q_c, k_c, v_c = linear_pre(xc, norm1_g, mods, ctx_row, w_in, segs, 512, name="att_in_ctx")
    o_l, o_c = attention_core(q_l, k_l, v_l, q_c, k_c, v_c, sink, batch)
    return _finish_layer(o_l, o_c, xl, xc, w_out, norm2_g, up, down, mods, rows)


def gdn_layer(xl, xc, mods, rows, batch, norm1_g, norm2_g, w_in, conv_w, a_log, dt_bias, norm_g,
              w_out, up, down):
    lat_row, ctx_row = rows
    conv_ch = 2 * GDN_K_HEADS * GDN_DK + GDN_V_HEADS * GDN_DV
    zw = GDN_V_HEADS * GDN_DV
    w_main = w_in[:, :conv_ch + zw].astype(BF16)
    w_gate = w_in[:, conv_ch + zw:].astype(BF16)
    segs = [(conv_ch, F32, _identity), (zw, BF16, _silu)]
    gsegs = [(4 * GDN_V_HEADS, F32, _identity)]
    mx_l, z_l = linear_pre(xl, norm1_g, mods, lat_row, w_main, segs, 1024, name="gdn_in_lat")
    mx_c, _ = linear_pre(xc, norm1_g, mods, ctx_row, w_main, segs, 1024, name="gdn_in_ctx")
    (ba_l,) = linear_pre(xl, norm1_g, mods, lat_row, w_gate, gsegs, 128, name="gdn_gate_lat")
    (ba_c,) = linear_pre(xc, norm1_g, mods, ctx_row, w_gate, gsegs, 128, name="gdn_gate_ctx")
    ba = jnp.concatenate([ba_c.reshape(batch, CTX_LEN, -1), ba_l.reshape(batch, SEQ, -1)], axis=1)
    ba = ba.reshape(batch, GDN_ROWS, 2, 2, GDN_K_HEADS, 2).transpose(0, 4, 2, 3, 5, 1)
    ba = ba.reshape(batch, GDN_K_HEADS, 8, GDN_ROWS)
    o_l = gdn_core(mx_l, z_l, mx_c, ba, conv_w, a_log, dt_bias, norm_g, batch)
    return _finish_layer(o_l, None, xl, xc, w_out, norm2_g, up, down, mods, rows)


def sgu_layer(xl, xc, mods, rows, norm1_g, norm2_g, w_in, ln_g, ln_b, w_s, b_s, w_out, up, down,
              final_g):
    lat_row, _ = rows
    segs = [(SGU_WIDTH, BF16, _gelu_erf), (SGU_WIDTH, F32, _gelu_erf)]
    u_l, v_l = linear_pre(xl, norm1_g, mods, lat_row, w_in.astype(BF16), segs, 1024, name="sgu_in_lat")
    o_l = sgu_core(u_l, v_l, ln_g, ln_b, w_s, b_s)
    return _finish_layer(o_l, None, xl, xc, w_out, norm2_g, up, down, mods, rows, final_g=final_g)


def kernel(x, c, ctx, c_ctx, mod_w, mod_b, norm1_g, norm2_g, mlp_up, mlp_down, final_g,
           ret_w_in, ret_gn_g, ret_w_out, att_w_in, att_sink, att_w_out,
           gdn_w_in, gdn_conv_w, gdn_a_log, gdn_dt_bias, gdn_norm_g, gdn_w_out,
           sgu_w_in, sgu_ln_g, sgu_ln_b, sgu_w_s, sgu_b_s, sgu_w_out):
    batch, seq, d = x.shape
    assert (seq, d, ctx.shape[1]) == (SEQ, D_MODEL, CTX_LEN) and batch < MOD_ROWS
    xl = x.reshape(batch * SEQ, d)
    xc = ctx.reshape(batch * CTX_LEN, d)

    cond = jnp.zeros((MOD_ROWS, d), F32).at[:batch].set(c).at[batch].set(c_ctx)
    mods_all = ada_mods(cond, mod_w, mod_b).reshape(DEPTH, MOD_ROWS, 1, 6 * d)
    rows = (_row_fn(SEQ, ROW_TILE, False, batch), _row_fn(CTX_LEN, ROW_TILE, True, batch))

    xl, xc = retention_layer(xl, xc, mods_all[0], rows, batch, norm1_g[0], norm2_g[0],
                             ret_w_in[0], ret_gn_g[0], ret_w_out[0], mlp_up[0], mlp_down[0])
    xl, xc = attention_layer(xl, xc, mods_all[1], rows, batch, norm1_g[1], norm2_g[1],
                             att_w_in[0], att_sink[0], att_w_out[0], mlp_up[1], mlp_down[1])
    xl, xc = gdn_layer(xl, xc, mods_all[2], rows, batch, norm1_g[2], norm2_g[2],
                       gdn_w_in[0], gdn_conv_w[0], gdn_a_log[0], gdn_dt_bias[0], gdn_norm_g[0],
                       gdn_w_out[0], mlp_up[2], mlp_down[2])
    xl, xc = sgu_layer(xl, xc, mods_all[3], rows, norm1_g[3], norm2_g[3],
                       sgu_w_in[0], sgu_ln_g[0], sgu_ln_b[0], sgu_w_s[0], sgu_b_s[0], sgu_w_out[0],
                       mlp_up[3], mlp_down[3], final_g)
    return xl.reshape(batch, SEQ, d)
```

```python
import functools
import math

import numpy as np
import jax
import jax.numpy as jnp
from jax import lax
from jax.experimental import pallas as pl
from jax.experimental.pallas import tpu as pltpu

F32 = jnp.float32
BF16 = jnp.bfloat16

D_MODEL = 2048
SEQ = 2048
CTX_LEN = 256
GRID_W = 64
DEPTH = 4
NORM_EPS = 1e-6
ROPE_BASE = 10000.0
MLP_HIDDEN = 4 * D_MODEL

RET_HEADS = 8
RET_DK = 256
RET_DV = 512
RET_CHUNK = 128

ATT_HEADS = 16
ATT_KV_HEADS = 4
ATT_HD = 128
ATT_GROUP = 4
WINDOW = 128
ATT_BLOCK = 128

GDN_K_HEADS = 16
GDN_V_HEADS = 32
GDN_DK = 128
GDN_DV = 128
GDN_CONV = 4
GDN_CHUNK = 64

SGU_CHUNK = 128
SGU_GROUPS = 8
SGU_WIDTH = 2 * D_MODEL

VMEM_LIMIT_BYTES = 56 * 1024 * 1024
MOD_ROWS = 16
ROW_TILE = 512
NEG_BIG = -1e30


def _cparams(*sem):
    return pltpu.CompilerParams(dimension_semantics=sem, vmem_limit_bytes=VMEM_LIMIT_BYTES)


def _silu(x):
    return x * jax.nn.sigmoid(x)


def _identity(x):
    return x


def _gelu_erf(x):
    return 0.5 * x * (1.0 + lax.erf(x * (2.0 ** -0.5)))


def _dot(a, b):
    return jnp.dot(a, b, preferred_element_type=F32)


def _dot_nt(a, b):
    return lax.dot_general(a, b, (((1,), (1,)), ((), ())), preferred_element_type=F32)


def _dot_tn(a, b):
    return lax.dot_general(a, b, (((0,), (0,)), ((), ())), preferred_element_type=F32)


def _mods_kernel(cond_ref, w_ref, b_ref, o_ref):
    a = _silu(cond_ref[...]).astype(BF16)
    o_ref[...] = _dot(a, w_ref[...].astype(BF16)) + b_ref[...]


def ada_mods(cond, mod_w, mod_b):
    depth, d, n = mod_w.shape
    tn = 1024
    return pl.pallas_call(
        _mods_kernel,
        grid=(depth, n // tn),
        in_specs=[
            pl.BlockSpec((MOD_ROWS, d), lambda l, j: (0, 0)),
            pl.BlockSpec((None, d, tn), lambda l, j: (l, 0, j)),
            pl.BlockSpec((None, 1, tn), lambda l, j: (l, 0, j)),
        ],
        out_specs=pl.BlockSpec((None, MOD_ROWS, tn), lambda l, j: (l, 0, j)),
        out_shape=jax.ShapeDtypeStruct((depth, MOD_ROWS, n), F32),
        compiler_params=_cparams("arbitrary", "arbitrary"),
        name="ada_mods",
    )(cond, mod_w, mod_b.reshape(depth, 1, n))


def _mod_spec(row_fn, chunk, tn=None):
    if tn is None:
        return pl.BlockSpec((None, 1, D_MODEL), lambda i, j: (row_fn(i), 0, chunk))
    per = D_MODEL // tn
    return pl.BlockSpec((None, 1, tn), lambda i, j: (row_fn(i), 0, chunk * per + j))


def _row_fn(rows_per_batch, tm, is_ctx, batch):
    if is_ctx:
        return lambda i: batch
    per = rows_per_batch // tm
    return lambda i: i // per


def _norm_mod(x, g, shift, scale):
    ms = jnp.mean(x * x, axis=-1, keepdims=True)
    y = x * lax.rsqrt(ms + NORM_EPS) * g
    return y * (1.0 + scale) + shift


def _linear_pre_kernel(x_ref, g_ref, sh_ref, sc_ref, w_ref, *rest, segs):
    out_refs = rest[:len(segs)]
    h_ref = rest[len(segs)]
    j = pl.program_id(1)

    @pl.when(j == 0)
    def _():
        h_ref[...] = _norm_mod(x_ref[...], g_ref[...], sh_ref[...], sc_ref[...]).astype(BF16)

    acc = _dot(h_ref[...], w_ref[...])
    start = 0
    for (ntiles, _, act), o_ref in zip(segs, out_refs):
        @pl.when((j >= start) & (j < start + ntiles))
        def _(o_ref=o_ref, act=act):
            o_ref[...] = act(acc).astype(o_ref.dtype)
        start += ntiles


def linear_pre(x, norm_g, mods, row_fn, w, segments, tn, tm=ROW_TILE, name="linear_pre"):
    m, d = x.shape
    n = w.shape[1]
    tm = min(tm, m)
    segs = []
    out_specs = []
    out_shapes = []
    start = 0
    for width, dtype, act in segments:
        nt = width // tn
        assert nt * tn == width
        segs.append((nt, dtype, act))
        out_specs.append(pl.BlockSpec(
            (tm, tn), lambda i, j, s=start, nt=nt: (i, jnp.clip(j - s, 0, nt - 1))))
        out_shapes.append(jax.ShapeDtypeStruct((m, width), dtype))
        start += nt
    assert start * tn == n
    outs = pl.pallas_call(
        functools.partial(_linear_pre_kernel, segs=tuple(segs)),
        grid=(m // tm, n // tn),
        in_specs=[
            pl.BlockSpec((tm, d), lambda i, j: (i, 0)),
            pl.BlockSpec((1, d), lambda i, j: (0, 0)),
            _mod_spec(row_fn, 0),
            _mod_spec(row_fn, 1),
            pl.BlockSpec((d, tn), lambda i, j: (0, j)),
        ],
        out_specs=out_specs,
        out_shape=out_shapes,
        scratch_shapes=[pltpu.VMEM((tm, d), BF16)],
        compiler_params=_cparams("arbitrary", "arbitrary"),
        name=name,
    )(x, norm_g.reshape(1, d), mods, mods, w)
    return outs


def _linear_post_kernel(a_ref, w_ref, res_ref, gate_ref, o_ref):
    o_ref[...] = res_ref[...] + gate_ref[...] * _dot(a_ref[...], w_ref[...])


def linear_post(a, w, res, mods, row_fn, gate_chunk, tn=1024, tm=ROW_TILE, name="linear_post"):
    m, k = a.shape
    n = w.shape[1]
    tm = min(tm, m)
    return pl.pallas_call(
        _linear_post_kernel,
        grid=(m // tm, n // tn),
        in_specs=[
            pl.BlockSpec((tm, k), lambda i, j: (i, 0)),
            pl.BlockSpec((k, tn), lambda i, j: (0, j)),
            pl.BlockSpec((tm, tn), lambda i, j: (i, j)),
            _mod_spec(row_fn, gate_chunk, tn),
        ],
        out_specs=pl.BlockSpec((tm, tn), lambda i, j: (i, j)),
        out_shape=jax.ShapeDtypeStruct((m, n), F32),
        compiler_params=_cparams("arbitrary", "arbitrary"),
        name=name,
    )(a, w, res, mods)


def _mlp_kernel(x_ref, g_ref, sh_ref, sc_ref, gate_ref, wu_ref, wd_ref, fg_ref, o_ref,
                h_ref, acc_ref, *, final_norm):
    j = pl.program_id(1)

    @pl.when(j == 0)
    def _():
        h_ref[...] = _norm_mod(x_ref[...], g_ref[...], sh_ref[...], sc_ref[...]).astype(BF16)
        acc_ref[...] = jnp.zeros_like(acc_ref)

    a = jnp.maximum(_dot(h_ref[...], wu_ref[...]), 0.0)
    acc_ref[...] += _dot((a * a).astype(BF16), wd_ref[...])

    @pl.when(j == pl.num_programs(1) - 1)
    def _():
        y = x_ref[...] + gate_ref[...] * acc_ref[...]
        if final_norm:
            ms = jnp.mean(y * y, axis=-1, keepdims=True)
            y = y * lax.rsqrt(ms + NORM_EPS) * fg_ref[...]
        o_ref[...] = y


def mlp_block(x, norm_g, mods, row_fn, w_up, w_down, final_g=None, th=1024, tm=ROW_TILE, name="mlp"):
    m, d = x.shape
    hid = w_up.shape[1]
    tm = min(tm, m)
    final_norm = final_g is not None
    fg = (final_g if final_norm else norm_g).reshape(1, d)
    return pl.pallas_call(
        functools.partial(_mlp_kernel, final_norm=final_norm),
        grid=(m // tm, hid // th),
        in_specs=[
            pl.BlockSpec((tm, d), lambda i, j: (i, 0)),
            pl.BlockSpec((1, d), lambda i, j: (0, 0)),
            _mod_spec(row_fn, 3),
            _mod_spec(row_fn, 4),
            _mod_spec(row_fn, 5),
            pl.BlockSpec((d, th), lambda i, j: (0, j)),
            pl.BlockSpec((th, d), lambda i, j: (j, 0)),
            pl.BlockSpec((1, d), lambda i, j: (0, 0)),
        ],
        out_specs=pl.BlockSpec((tm, d), lambda i, j: (i, 0)),
        out_shape=jax.ShapeDtypeStruct((m, d), F32),
        scratch_shapes=[pltpu.VMEM((tm, d), BF16), pltpu.VMEM((tm, d), F32)],
        compiler_params=_cparams("arbitrary", "arbitrary"),
        name=name,
    )(x, norm_g.reshape(1, d), mods, mods, mods, w_up, w_down, fg)


def rope_tables(head_dim):
    rows = SEQ // GRID_W
    row = jnp.repeat(jnp.arange(rows, dtype=F32), GRID_W)
    col = jnp.tile(jnp.arange(GRID_W, dtype=F32), rows)
    axis_dim = head_dim // 2
    inv_freq = jnp.exp(-math.log(ROPE_BASE) * jnp.arange(0, axis_dim, 2, dtype=F32) / axis_dim)
    ang = jnp.concatenate([row[:, None] * inv_freq, col[:, None] * inv_freq], axis=-1)
    cos = jnp.repeat(jnp.cos(ang), 2, axis=-1)
    sin = jnp.repeat(jnp.sin(ang), 2, axis=-1)
    sign = jnp.where(jnp.arange(head_dim) % 2 == 0, -1.0, 1.0).astype(F32)
    return cos, sin * sign


def _rope(x, cos, sin_signed):
    width = x.shape[-1]
    nxt = pltpu.roll(x, width - 1, 1)
    prv = pltpu.roll(x, 1, 1)
    lane = lax.broadcasted_iota(jnp.int32, x.shape, 1)
    partner = jnp.where(lane % 2 == 0, nxt, prv)
    return x * cos + partner * sin_signed


def _ret_tables():
    lg = np.log1p(-np.exp2(-5.0 - np.arange(RET_HEADS, dtype=np.float64)))[:, None]
    pos = np.arange(RET_CHUNK, dtype=np.float64)
    dist = np.abs(pos[:, None] - pos[None, :])
    intra = np.exp(lg[:, :, None] * dist) * (1.0 + np.eye(RET_CHUNK))
    q_f = np.exp(lg * (pos + 1.0))
    q_b = np.exp(lg * (RET_CHUNK - pos))
    k_f = np.exp(lg * (RET_CHUNK - 1.0 - pos))
    k_b = np.exp(lg * pos)
    dec = np.stack([q_f, q_b, k_f, k_b], axis=1)
    dec = np.broadcast_to(dec[..., None], dec.shape + (RET_DK,))
    cdec = np.broadcast_to(np.exp(lg * RET_CHUNK)[:, :, None], (RET_HEADS, 8, RET_DV))
    return (jnp.asarray(intra, F32), jnp.asarray(dec, F32), jnp.asarray(cdec, F32))


def _ret_kernel(ql_ref, kl_ref, vl_ref, gl_ref, qc_ref, kc_ref, vc_ref, gc_ref,
                cos_ref, sin_ref, intra_ref, dec_ref, cdec_ref, gn_ref,
                ol_ref, oc_ref, qs_ref, ks_ref, sb_ref, st_ref):
    C = RET_CHUNK
    n_lat = SEQ // C
    n_ctx = CTX_LEN // C
    kscale = RET_DK ** -0.5
    cd = cdec_ref[0:1, :]

    def rope_chunk(c, carry):
        rows = pl.ds(pl.multiple_of(c * C, C), C)
        cs = cos_ref[rows, :]
        sn = sin_ref[rows, :]
        qs_ref[rows, :] = _rope(ql_ref[rows, :], cs, sn)
        ks_ref[rows, :] = _rope(kl_ref[rows, :], cs, sn) * kscale
        return carry

    lax.fori_loop(0, n_lat, rope_chunk, 0)

    st_ref[...] = jnp.zeros_like(st_ref)

    def back_step(k, v, slot):
        sb_ref[slot] = st_ref[...].astype(BF16)
        kd = (k * dec_ref[3]).astype(BF16)
        st_ref[...] = st_ref[...] * cd + _dot_tn(kd, v)

    for c in reversed(range(n_ctx)):
        rows = pl.ds(c * C, C)
        back_step(kc_ref[rows, :] * kscale, vc_ref[rows, :], c)

    def back_lat(t, carry):
        c = n_lat - 1 - t
        rows = pl.ds(pl.multiple_of(c * C, C), C)
        back_step(ks_ref[rows, :], vl_ref[rows, :], n_ctx + c)
        return carry

    lax.fori_loop(0, n_lat, back_lat, 0)

    st_ref[...] = jnp.zeros_like(st_ref)

    def fwd_step(q, k, v, gate, slot):
        s = _dot_nt(q.astype(BF16), k.astype(BF16)) * intra_ref[...]
        st = st_ref[...]
        o = (_dot(s.astype(BF16), v)
             + _dot((q * dec_ref[0]).astype(BF16), st.astype(BF16))
             + _dot((q * dec_ref[1]).astype(BF16), sb_ref[slot]))
        st_ref[...] = st * cd + _dot_tn((k * dec_ref[2]).astype(BF16), v)
        ms = jnp.mean(o * o, axis=-1, keepdims=True)
        on = o * lax.rsqrt(ms + NORM_EPS) * gn_ref[...]
        return (on * gate.astype(F32)).astype(BF16)

    for c in range(n_ctx):
        rows = pl.ds(c * C, C)
        oc_ref[rows, :] = fwd_step(qc_ref[rows, :], kc_ref[rows, :] * kscale, vc_ref[rows, :],
                                   gc_ref[rows, :], c)

    def fwd_lat(c, carry):
        rows = pl.ds(pl.multiple_of(c * C, C), C)
        ol_ref[rows, :] = fwd_step(qs_ref[rows, :], ks_ref[rows, :], vl_ref[rows, :],
                                   gl_ref[rows, :], n_ctx + c)
        return carry

    lax.fori_loop(0, n_lat, fwd_lat, 0)


def retention_core(qk_lat, v_lat, g_lat, qk_ctx, v_ctx, g_ctx, gn_g, batch):
    H = RET_HEADS
    cos, sin = rope_tables(RET_DK)
    intra, dec, cdec = _ret_tables()
    n_chunks = (SEQ + CTX_LEN) // RET_CHUNK

    def specs(rows):
        return [
            pl.BlockSpec((rows, RET_DK), lambda b, h: (b, h)),
            pl.BlockSpec((rows, RET_DK), lambda b, h: (b, H + h)),
            pl.BlockSpec((rows, RET_DV), lambda b, h: (b, h)),
            pl.BlockSpec((rows, RET_DV), lambda b, h: (b, h)),
        ]

    return pl.pallas_call(
        _ret_kernel,
        grid=(batch, H),
        in_specs=specs(SEQ) + specs(CTX_LEN) + [
            pl.BlockSpec((SEQ, RET_DK), lambda b, h: (0, 0)),
            pl.BlockSpec((SEQ, RET_DK), lambda b, h: (0, 0)),
            pl.BlockSpec((None, RET_CHUNK, RET_CHUNK), lambda b, h: (h, 0, 0)),
            pl.BlockSpec((None, 4, RET_CHUNK, RET_DK), lambda b, h: (h, 0, 0, 0)),
            pl.BlockSpec((None, 8, RET_DV), lambda b, h: (h, 0, 0)),
            pl.BlockSpec((1, RET_DV), lambda b, h: (0, h)),
        ],
        out_specs=[
            pl.BlockSpec((SEQ, RET_DV), lambda b, h: (b, h)),
            pl.BlockSpec((CTX_LEN, RET_DV), lambda b, h: (b, h)),
        ],
        out_shape=[
            jax.ShapeDtypeStruct((batch * SEQ, H * RET_DV), BF16),
            jax.ShapeDtypeStruct((batch * CTX_LEN, H * RET_DV), BF16),
        ],
        scratch_shapes=[
            pltpu.VMEM((SEQ, RET_DK), F32),
            pltpu.VMEM((SEQ, RET_DK), F32),
            pltpu.VMEM((n_chunks, RET_DK, RET_DV), BF16),
            pltpu.VMEM((RET_DK, RET_DV), F32),
        ],
        compiler_params=_cparams("arbitrary", "arbitrary"),
        name="retention_core",
    )(qk_lat, qk_lat, v_lat, g_lat, qk_ctx, qk_ctx, v_ctx, g_ctx,
      cos, sin, intra, dec, cdec, gn_g.reshape(1, H * RET_DV))


def _att_kernel(sink_ref, ql_ref, kl_ref, vl_ref, qc_ref, kc_ref, vc_ref, cos_ref, sin_ref,
                ol_ref, oc_ref, kp_ref, vp_ref):
    BLK = ATT_BLOCK
    G = ATT_GROUP
    HD = ATT_HD
    n_blk = SEQ // BLK
    span = BLK + 2 * WINDOW
    scale = HD ** -0.5
    kvh = pl.program_id(1)

    zpad = jnp.zeros((WINDOW, HD), BF16)
    kp_ref[pl.ds(0, WINDOW), :] = zpad
    vp_ref[pl.ds(0, WINDOW), :] = zpad
    kp_ref[pl.ds(WINDOW + SEQ, WINDOW), :] = zpad
    vp_ref[pl.ds(WINDOW + SEQ, WINDOW), :] = zpad
    ctx0 = SEQ + 2 * WINDOW
    kp_ref[pl.ds(ctx0, CTX_LEN), :] = kc_ref[...].astype(BF16)
    vp_ref[pl.ds(ctx0, CTX_LEN), :] = vc_ref[...]

    def stage(c, carry):
        rows = pl.ds(pl.multiple_of(c * BLK, BLK), BLK)
        dst = pl.ds(pl.multiple_of(c * BLK + WINDOW, BLK), BLK)
        kp_ref[dst, :] = _rope(kl_ref[rows, :], cos_ref[rows, :], sin_ref[rows, :]).astype(BF16)
        vp_ref[dst, :] = vl_ref[rows, :]
        return carry

    lax.fori_loop(0, n_blk, stage, 0)

    ridx = lax.broadcasted_iota(jnp.int32, (G * BLK, 1), 0)
    grp = ridx // BLK
    sink_col = jnp.zeros((G * BLK, 1), F32)
    for g in range(G):
        sink_col = jnp.where(grp == g, sink_ref[kvh * G + g], sink_col)
    kc_b = kp_ref[pl.ds(ctx0, CTX_LEN), :]
    vc_b = vp_ref[pl.ds(ctx0, CTX_LEN), :]

    def finish(parts, out_ref, rows):
        m = sink_col
        for s, _ in parts:
            m = jnp.maximum(m, jnp.max(s, axis=-1, keepdims=True))
        den = jnp.exp(sink_col - m)
        acc = None
        for s, v in parts:
            p = jnp.exp(s - m)
            den = den + jnp.sum(p, axis=-1, keepdims=True)
            pv = _dot(p.astype(BF16), v)
            acc = pv if acc is None else acc + pv
        o = acc / den
        for g in range(G):
            out_ref[rows, pl.ds(g * HD, HD)] = o[g * BLK:(g + 1) * BLK, :].astype(BF16)

    qi = ridx % BLK
    col = lax.broadcasted_iota(jnp.int32, (G * BLK, span), 1)
    band = (col >= qi) & (col <= qi + 2 * WINDOW)

    def lat_block(bi, carry):
        rows = pl.ds(pl.multiple_of(bi * BLK, BLK), BLK)
        cs = cos_ref[rows, :]
        sn = sin_ref[rows, :]
        qs = [(_rope(ql_ref[rows, pl.ds(g * HD, HD)], cs, sn) * scale).astype(BF16) for g in range(G)]
        q4 = jnp.concatenate(qs, axis=0)
        krows = pl.ds(pl.multiple_of(bi * BLK, BLK), span)
        s_lat = _dot_nt(q4, kp_ref[krows, :])
        kpos = col + (bi * BLK - WINDOW)
        valid = band & (kpos >= 0) & (kpos < SEQ)
        s_lat = jnp.where(valid, s_lat, NEG_BIG)
        s_ctx = _dot_nt(q4, kc_b)
        finish([(s_lat, vp_ref[krows, :]), (s_ctx, vc_b)], ol_ref, rows)
        return carry

    lax.fori_loop(0, n_blk, lat_block, 0)

    for bi in range(CTX_LEN // BLK):
        rows = pl.ds(bi * BLK, BLK)
        qs = [(qc_ref[rows, pl.ds(g * HD, HD)] * scale).astype(BF16) for g in range(G)]
        q4 = jnp.concatenate(qs, axis=0)
        finish([(_dot_nt(q4, kc_b), vc_b)], oc_ref, rows)


def attention_core(q_lat, k_lat, v_lat, q_ctx, k_ctx, v_ctx, sink, batch):
    cos, sin = rope_tables(ATT_HD)
    GW = ATT_GROUP * ATT_HD
    pad_rows = SEQ + 2 * WINDOW + CTX_LEN

    def specs(rows):
        return [
            pl.BlockSpec((rows, GW), lambda b, h: (b, h)),
            pl.BlockSpec((rows, ATT_HD), lambda b, h: (b, h)),
            pl.BlockSpec((rows, ATT_HD), lambda b, h: (b, h)),
        ]

    return pl.pallas_call(
        _att_kernel,
        grid=(batch, ATT_KV_HEADS),
        in_specs=[pl.BlockSpec(memory_space=pltpu.SMEM)] + specs(SEQ) + specs(CTX_LEN) + [
            pl.BlockSpec((SEQ, ATT_HD), lambda b, h: (0, 0)),
            pl.BlockSpec((SEQ, ATT_HD), lambda b, h: (0, 0)),
        ],
        out_specs=[
            pl.BlockSpec((SEQ, GW), lambda b, h: (b, h)),
            pl.BlockSpec((CTX_LEN, GW), lambda b, h: (b, h)),
        ],
        out_shape=[
            jax.ShapeDtypeStruct((batch * SEQ, ATT_HEADS * ATT_HD), BF16),
            jax.ShapeDtypeStruct((batch * CTX_LEN, ATT_HEADS * ATT_HD), BF16),
        ],
        scratch_shapes=[
            pltpu.VMEM((pad_rows, ATT_HD), BF16),
            pltpu.VMEM((pad_rows, ATT_HD), BF16),
        ],
        compiler_params=_cparams("arbitrary", "arbitrary"),
        name="attention_core",
    )(sink.astype(F32), q_lat, k_lat, v_lat, q_ctx, k_ctx, v_ctx, cos, sin)


def _sgu_kernel(u_ref, v_ref, g_ref, b_ref, ws_ref, bs_ref, o_ref, *, chunks):
    C = SGU_CHUNK
    gw = SGU_WIDTH // SGU_GROUPS
    for c in range(chunks):
        rows = pl.ds(c * C, C)
        v = v_ref[rows, :]
        mu = jnp.mean(v, axis=-1, keepdims=True)
        xc = v - mu
        var = jnp.mean(xc * xc, axis=-1, keepdims=True)
        vn = (xc * lax.rsqrt(var + NORM_EPS) * g_ref[...] + b_ref[...]).astype(BF16)
        for g in range(SGU_GROUPS):
            cols = pl.ds(g * gw, gw)
            t = _dot(ws_ref[g], vn[:, g * gw:(g + 1) * gw]) + bs_ref[g]
            o_ref[rows, cols] = (u_ref[rows, cols].astype(F32) * t).astype(BF16)


def sgu_core(u, v, ln_g, ln_b, w_s, b_s, chunks=2):
    m = u.shape[0]
    tm = chunks * SGU_CHUNK
    return pl.pallas_call(
        functools.partial(_sgu_kernel, chunks=chunks),
        grid=(m // tm,),
        in_specs=[
            pl.BlockSpec((tm, SGU_WIDTH), lambda i: (i, 0)),
            pl.BlockSpec((tm, SGU_WIDTH), lambda i: (i, 0)),
            pl.BlockSpec((1, SGU_WIDTH), lambda i: (0, 0)),
            pl.BlockSpec((1, SGU_WIDTH), lambda i: (0, 0)),
            pl.BlockSpec((SGU_GROUPS, SGU_CHUNK, SGU_CHUNK), lambda i: (0, 0, 0)),
            pl.BlockSpec((SGU_GROUPS, SGU_CHUNK, 1), lambda i: (0, 0, 0)),
        ],
        out_specs=pl.BlockSpec((tm, SGU_WIDTH), lambda i: (i, 0)),
        out_shape=jax.ShapeDtypeStruct((m, SGU_WIDTH), BF16),
        compiler_params=_cparams("arbitrary"),
        name="sgu_core",
    )(u, v, ln_g.reshape(1, -1), ln_b.reshape(1, -1), w_s.astype(BF16),
      b_s.reshape(SGU_GROUPS, SGU_CHUNK, 1))


GDN_ROWS = CTX_LEN + SEQ
GDN_NCHUNK = GDN_ROWS // GDN_CHUNK
GDN_CONV_ROWS = 256
GDN_HALO = 8
GDN_PREP_TILES = 3


def _gdn_kernel(ql_ref, kl_ref, vl_ref, zl_ref, qc_ref, kc_ref, vc_ref, ba_ref,
                cwq_ref, cwk_ref, cwv_ref, alog_ref, dt_ref, ng_ref,
                ol_ref,
                xp_ref, qn_ref, kn_ref, vv_ref, gt_ref, qp_ref, g_ref, b_ref,
                el_ref, of_ref, ob_ref, s_ref):
    C = GDN_CHUNK
    DK = GDN_DK
    R = GDN_CONV_ROWS
    HALO = GDN_HALO

    def conv_silu(src_ref, n_rows, cw_ref, width, store):
        cols = pl.ds(0, width)
        xp_ref[pl.ds(0, HALO), cols] = jnp.zeros((HALO, width), F32)
        xp_ref[pl.ds(HALO, n_rows), cols] = src_ref[...]
        xp_ref[pl.ds(HALO + n_rows, HALO), cols] = jnp.zeros((HALO, width), F32)
        w = cw_ref[...]
        n = R + 2 * HALO

        def body(c, carry):
            r0 = pl.multiple_of(c * R, R)
            x = xp_ref[pl.ds(r0, n), cols]
            y = (w[2:3] * x + w[1:2] * pltpu.roll(x, 1, 0) + w[0:1] * pltpu.roll(x, 2, 0)
                 + w[3:4] * pltpu.roll(x, n - 1, 0))[HALO:HALO + R]
            store(r0, _silu(y))
            return carry

        lax.fori_loop(0, n_rows // R, body, 0)

    def unit(y):
        return y * lax.rsqrt(jnp.sum(y * y, axis=-1, keepdims=True) + 1e-6)

    def store_q(off):
        def f(r0, y):
            qn_ref[pl.ds(pl.multiple_of(off + r0, R), R), :] = unit(y) * (DK ** -0.5)
        return f

    def store_k(off):
        def f(r0, y):
            kn_ref[pl.ds(pl.multiple_of(off + r0, R), R), :] = unit(y)
        return f

    def store_v(off):
        def f(r0, y):
            vv_ref[pl.ds(pl.multiple_of(off + r0, R), R), :] = y
        return f

    conv_silu(qc_ref, CTX_LEN, cwq_ref, DK, store_q(0))
    conv_silu(kc_ref, CTX_LEN, cwk_ref, DK, store_k(0))
    conv_silu(vc_ref, CTX_LEN, cwv_ref, 2 * GDN_DV, store_v(0))
    conv_silu(ql_ref, SEQ, cwq_ref, DK, store_q(CTX_LEN))
    conv_silu(kl_ref, SEQ, cwk_ref, DK, store_k(CTX_LEN))
    conv_silu(vl_ref, SEQ, cwv_ref, 2 * GDN_DV, store_v(CTX_LEN))

    raw = ba_ref[...]
    xg = raw + dt_ref[...]
    softplus = jnp.maximum(xg, 0.0) + jnp.log1p(jnp.exp(-jnp.abs(xg)))
    g = -jnp.exp(alog_ref[...]) * softplus
    pos = lax.broadcasted_iota(jnp.int32, raw.shape, 1) % C
    pre = g
    suf = g
    shift = 1
    while shift < C:
        pre = pre + jnp.where(pos >= shift, pltpu.roll(pre, shift, 1), 0.0)
        suf = suf + jnp.where(pos + shift < C, pltpu.roll(suf, GDN_ROWS - shift, 1), 0.0)
        shift *= 2
    row = lax.broadcasted_iota(jnp.int32, raw.shape, 0)
    gt_ref[...] = jnp.where(row < 4, jax.nn.sigmoid(raw), jnp.where(row < 6, pre, suf))

    PK = 2 * C
    ri = lax.broadcasted_iota(jnp.int32, (C, PK), 0)
    li = lax.broadcasted_iota(jnp.int32, (C, PK), 1)
    ci = li % C
    left = li < C
    eye = (ri == ci).astype(F32)
    incl = (ri >= ci, ri <= ci)
    strict = (ri > ci, ri < ci)
    half_sel = (li == ri, li == ri + C)

    def to_col(rowvec, half):
        return jnp.sum(jnp.where(half_sel[half], rowvec, 0.0), axis=1, keepdims=True)

    def block_diag(m, zero):
        return jnp.concatenate([jnp.concatenate([m[0], zero], axis=1),
                                jnp.concatenate([zero, m[1]], axis=1)], axis=0)

    def dot3_packed(x, m):
        bd = jnp.concatenate([jnp.where(left, m, 0.0), jnp.where(left, 0.0, m)], axis=0)
        x_hi = x.astype(BF16)
        bd_hi = bd.astype(BF16)
        x_lo = (x - x_hi.astype(F32)).astype(BF16)
        bd_lo = (bd - bd_hi.astype(F32)).astype(BF16)
        rhs = jnp.concatenate([jnp.concatenate([bd_hi, bd_lo], axis=1),
                               jnp.concatenate([bd_hi, jnp.zeros_like(bd_hi)], axis=1)], axis=0)
        out = _dot(jnp.concatenate([x_hi, x_lo], axis=1), rhs)
        return out[:, :PK] + out[:, PK:]

    def prep_pair(p, carry):
        zero_uw = jnp.zeros((C, 2 * GDN_DV), BF16)
        chains = []
        for sub in range(2 * GDN_PREP_TILES):
            tile, half = divmod(sub, 2)
            if half == 0:
                lanes = pl.ds(pl.multiple_of((p * GDN_PREP_TILES + tile) * PK, PK), PK)
                gates = gt_ref[:, lanes]
                gates_sw = pltpu.roll(gates, C, 1)
            chunk = (p * GDN_PREP_TILES + tile) * 2 + half
            rows = pl.ds(pl.multiple_of(chunk * C, C), C)
            q = qn_ref[rows, :]
            k = kn_ref[rows, :]
            v2 = vv_ref[rows, :]
            qk_kk = _dot_nt(jnp.concatenate([q, k], axis=0).astype(BF16),
                            jnp.concatenate([k, k], axis=0).astype(BF16))
            for d in range(2):
                def packed_row(r, half=half, d=d):
                    a = (gates if half == 0 else gates_sw)[r + 2 * d:r + 2 * d + 1]
                    b = (gates_sw if half == 0 else gates)[r + 2 * d + 1:r + 2 * d + 2]
                    return jnp.where(left[0:1], a, b)
                edge = half * C + (C - 1 if d == 0 else 0)
                chains.append(dict(
                    d=d, n=chunk, rows=rows, q=q, k=k, v2=v2, qk=qk_kk[:C], kk=qk_kk[C:],
                    grow=packed_row(4),
                    bcols=[to_col(gates[2 * d + j:2 * d + j + 1], half) for j in range(2)],
                    gcols=[to_col(gates[4 + 2 * d + j:5 + 2 * d + j], half) for j in range(2)],
                    glast=[gates[4 + 2 * d + j:5 + 2 * d + j, edge:edge + 1] for j in range(2)]))
        for ch in chains:
            d = ch["d"]
            bcol = jnp.where(left, ch["bcols"][0], ch["bcols"][1])
            gcol = jnp.where(left, ch["gcols"][0], ch["gcols"][1])
            ch["dec"] = jnp.exp(jnp.where(incl[d], gcol - ch["grow"], NEG_BIG))
            nmat = jnp.where(strict[d], -(ch["kk"] * bcol * ch["dec"]), 0.0)
            ch["pm"] = eye + nmat
            ch["nmat"] = nmat
        for ch in chains:
            ch["mm"] = dot3_packed(ch["nmat"], ch["nmat"])
        for _ in range(4):
            for ch in chains:
                r = dot3_packed(jnp.concatenate([ch["pm"], ch["mm"]], axis=0), ch["mm"])
                ch["pm"] = ch["pm"] + r[:C]
                ch["mm"] = r[C:]
        for ch in chains:
            ch["r"] = dot3_packed(ch["pm"], ch["mm"])
        for ch in chains:
            tmat = (ch["pm"] + ch["r"]).astype(BF16)
            ch["eg"] = [jnp.exp(g) for g in ch["gcols"]]
            rhs = [jnp.concatenate([ch["v2"][:, j * GDN_DV:(j + 1) * GDN_DV] * ch["bcols"][j],
                                    ch["k"] * (ch["bcols"][j] * ch["eg"][j])], axis=1).astype(BF16)
                   for j in range(2)]
            ch["uw"] = _dot(tmat, block_diag(rhs, zero_uw)).astype(BF16)
        for ch in chains:
            uw = [ch["uw"][:, j * 2 * GDN_DV:(j + 1) * 2 * GDN_DV] for j in range(2)]
            amat = (ch["qk"] * ch["dec"]).astype(BF16)
            ch["a_uw"] = _dot(amat, block_diag(uw, zero_uw))
            ch["k_uw"] = [_dot_tn((ch["k"] * jnp.exp(ch["glast"][j] - ch["gcols"][j])).astype(BF16), uw[j])
                          for j in range(2)]
        for ch in chains:
            n, rows = ch["n"], ch["rows"]
            out_ref = of_ref if ch["d"] == 0 else ob_ref
            for j in range(2):
                combo = 2 * ch["d"] + j
                a_u = ch["a_uw"][:, (2 * j) * GDN_DV:(2 * j + 1) * GDN_DV]
                a_w = ch["a_uw"][:, (2 * j + 1) * GDN_DV:(2 * j + 2) * GDN_DV]
                qp_ref[combo, rows, :] = (ch["q"] * ch["eg"][j] - a_w).astype(BF16)
                out_ref[rows, pl.ds(j * GDN_DV, GDN_DV)] = a_u
                b_ref[combo, n] = ch["k_uw"][j][:, :GDN_DV]
                g_ref[combo, n] = ch["k_uw"][j][:, GDN_DV:].astype(BF16)
                el_ref[combo, n] = jnp.broadcast_to(jnp.exp(ch["glast"][j]), (8, GDN_DV))
        return carry

    lax.fori_loop(0, GDN_NCHUNK // (2 * GDN_PREP_TILES), prep_pair, 0)

    s_ref[...] = jnp.zeros_like(s_ref)

    def rec_pair(nf, nb):
        work = []
        for combo in range(4):
            n = nf if combo < 2 else nb
            rows = pl.ds(pl.multiple_of(n * C, C), C)
            out_ref = of_ref if combo < 2 else ob_ref
            cols = pl.ds((combo % 2) * GDN_DV, GDN_DV)
            lhs = jnp.concatenate([g_ref[combo, n], qp_ref[combo, rows, :]], axis=0)
            work.append((combo, out_ref, rows, cols, lhs, s_ref[combo], b_ref[combo, n],
                         el_ref[combo, n][0:1, :], out_ref[rows, cols]))
        done = []
        for combo, out_ref, rows, cols, lhs, st, bmat, e_last, o0 in work:
            r = _dot(lhs, st.astype(BF16))
            done.append((combo, out_ref, rows, cols, st * e_last + bmat - r[:GDN_DK], o0 + r[GDN_DK:]))
        for combo, out_ref, rows, cols, st_new, o in done:
            s_ref[combo] = st_new
            out_ref[rows, cols] = o

    n_ctx = CTX_LEN // C
    for t in range(n_ctx):
        rec_pair(t, n_ctx - 1 - t)

    def rec_lat(t, carry):
        rec_pair(n_ctx + t, GDN_NCHUNK - 1 - t)
        return carry

    lax.fori_loop(0, SEQ // C, rec_lat, 0)

    def fin(c, carry):
        src = pl.ds(pl.multiple_of(CTX_LEN + c * R, R), R)
        dst = pl.ds(pl.multiple_of(c * R, R), R)
        for j in range(2):
            cols = pl.ds(j * GDN_DV, GDN_DV)
            o = of_ref[src, cols] + ob_ref[src, cols]
            ms = jnp.mean(o * o, axis=-1, keepdims=True)
            on = o * lax.rsqrt(ms + NORM_EPS) * ng_ref[...]
            ol_ref[dst, cols] = (on * zl_ref[dst, cols].astype(F32)).astype(BF16)
        return carry

    lax.fori_loop(0, SEQ // R, fin, 0)


def gdn_core(mixed_lat, z_lat, mixed_ctx, ba, conv_w, a_log, dt_bias, norm_g, batch):
    HK = GDN_K_HEADS
    DK = GDN_DK
    VW = 2 * GDN_DV
    v_blk0 = (2 * HK * DK) // VW

    def tab(t):
        t = t.reshape(2, HK, 2).transpose(1, 0, 2).reshape(HK, 4)
        return jnp.concatenate([jnp.zeros_like(t), t], axis=1).reshape(HK, 8, 1).astype(F32)

    def specs(rows):
        return [
            pl.BlockSpec((rows, DK), lambda b, h: (b, h)),
            pl.BlockSpec((rows, DK), lambda b, h: (b, HK + h)),
            pl.BlockSpec((rows, VW), lambda b, h: (b, v_blk0 + h)),
        ]

    nc = GDN_NCHUNK
    return pl.pallas_call(
        _gdn_kernel,
        grid=(batch, HK),
        in_specs=specs(SEQ) + [pl.BlockSpec((SEQ, VW), lambda b, h: (b, h))] + specs(CTX_LEN) + [
            pl.BlockSpec((None, None, 8, GDN_ROWS), lambda b, h: (b, h, 0, 0)),
            pl.BlockSpec((GDN_CONV, DK), lambda b, h: (0, h)),
            pl.BlockSpec((GDN_CONV, DK), lambda b, h: (0, HK + h)),
            pl.BlockSpec((GDN_CONV, VW), lambda b, h: (0, v_blk0 + h)),
            pl.BlockSpec((None, 8, 1), lambda b, h: (h, 0, 0)),
            pl.BlockSpec((None, 8, 1), lambda b, h: (h, 0, 0)),
            pl.BlockSpec((1, GDN_DV), lambda b, h: (0, 0)),
        ],
        out_specs=pl.BlockSpec((SEQ, VW), lambda b, h: (b, h)),
        out_shape=jax.ShapeDtypeStruct((batch * SEQ, GDN_V_HEADS * GDN_DV), BF16),
        scratch_shapes=[
            pltpu.VMEM((SEQ + 2 * GDN_HALO, VW), F32),
            pltpu.VMEM((GDN_ROWS, DK), F32),
            pltpu.VMEM((GDN_ROWS, DK), F32),
            pltpu.VMEM((GDN_ROWS, VW), F32),
            pltpu.VMEM((8, GDN_ROWS), F32),
            pltpu.VMEM((4, GDN_ROWS, DK), BF16),
            pltpu.VMEM((4, nc, DK, GDN_DV), BF16),
            pltpu.VMEM((4, nc, DK, GDN_DV), F32),
            pltpu.VMEM((4, nc, 8, GDN_DV), F32),
            pltpu.VMEM((GDN_ROWS, VW), F32),
            pltpu.VMEM((GDN_ROWS, VW), F32),
            pltpu.VMEM((4, DK, GDN_DV), F32),
        ],
        compiler_params=_cparams("arbitrary", "arbitrary"),
        name="gdn_core",
    )(mixed_lat, mixed_lat, mixed_lat, z_lat, mixed_ctx, mixed_ctx, mixed_ctx, ba,
      conv_w, conv_w, conv_w, tab(a_log), tab(dt_bias), norm_g.reshape(1, GDN_DV))


def _finish_layer(o_lat, o_ctx, xl, xc, w_out, norm2_g, up, down, mods, rows, final_g=None):
    lat_row, ctx_row = rows
    w_out = w_out.astype(BF16)
    up = up.astype(BF16)
    down = down.astype(BF16)
    xl = linear_post(o_lat, w_out, xl, mods, lat_row, 2, name="out_proj_lat")
    xl = mlp_block(xl, norm2_g, mods, lat_row, up, down, final_g=final_g, name="mlp_lat")
    if o_ctx is not None:
        xc = linear_post(o_ctx, w_out, xc, mods, ctx_row, 2, name="out_proj_ctx")
        xc = mlp_block(xc, norm2_g, mods, ctx_row, up, down, name="mlp_ctx")
    return xl, xc


def retention_layer(xl, xc, mods, rows, batch, norm1_g, norm2_g, w_in, gn_g, w_out, up, down):
    lat_row, ctx_row = rows
    w_in = w_in.astype(BF16)
    qkw = 2 * RET_HEADS * RET_DK
    vw = RET_HEADS * RET_DV
    segs = [(qkw, F32, _identity), (vw, BF16, _identity), (vw, BF16, _silu)]
    qk_l, v_l, g_l = linear_pre(xl, norm1_g, mods, lat_row, w_in, segs, 1024, name="ret_in_lat")
    qk_c, v_c, g_c = linear_pre(xc, norm1_g, mods, ctx_row, w_in, segs, 1024, name="ret_in_ctx")
    o_l, o_c = retention_core(qk_l, v_l, g_l, qk_c, v_c, g_c, gn_g, batch)
    return _finish_layer(o_l, o_c, xl, xc, w_out, norm2_g, up, down, mods, rows)


def attention_layer(xl, xc, mods, rows, batch, norm1_g, norm2_g, w_in, sink, w_out, up, down):
    lat_row, ctx_row = rows
    w_in = w_in.astype(BF16)
    qw = ATT_HEADS * ATT_HD
    kvw = ATT_KV_HEADS * ATT_HD
    segs = [(qw, F32, _identity), (kvw, F32, _identity), (kvw, BF16, _identity)]
    q_l, k_l, v_l = linear_pre(xl, norm1_g, mods, lat_row, w_in, segs, 512, name="att_in_lat")
    q_c, k_c, v_c = linear_pre(xc, norm1_g, mods, ctx_row, w_in, segs, 512, name="att_in_ctx")
    o_l, o_c = attention_core(q_l, k_l, v_l, q_c, k_c, v_c, sink, batch)
    return _finish_layer(o_l, o_c, xl, xc, w_out, norm2_g, up, down, mods, rows)


def gdn_layer(xl, xc, mods, rows, batch, norm1_g, norm2_g, w_in, conv_w, a_log, dt_bias, norm_g,
              w_out, up, down):
    lat_row, ctx_row = rows
    conv_ch = 2 * GDN_K_HEADS * GDN_DK + GDN_V_HEADS * GDN_DV
    zw = GDN_V_HEADS * GDN_DV
    w_main = w_in[:, :conv_ch + zw].astype(BF16)
    w_gate = w_in[:, conv_ch + zw:].astype(BF16)
    segs = [(conv_ch, F32, _identity), (zw, BF16, _silu)]
    gsegs = [(4 * GDN_V_HEADS, F32, _identity)]
    mx_l, z_l = linear_pre(xl, norm1_g, mods, lat_row, w_main, segs, 1024, name="gdn_in_lat")
    mx_c, _ = linear_pre(xc, norm1_g, mods, ctx_row, w_main, segs, 1024, name="gdn_in_ctx")
    (ba_l,) = linear_pre(xl, norm1_g, mods, lat_row, w_gate, gsegs, 128, name="gdn_gate_lat")
    (ba_c,) = linear_pre(xc, norm1_g, mods, ctx_row, w_gate, gsegs, 128, name="gdn_gate_ctx")
    ba = jnp.concatenate([ba_c.reshape(batch, CTX_LEN, -1), ba_l.reshape(batch, SEQ, -1)], axis=1)
    ba = ba.reshape(batch, GDN_ROWS, 2, 2, GDN_K_HEADS, 2).transpose(0, 4, 2, 3, 5, 1)
    ba = ba.reshape(batch, GDN_K_HEADS, 8, GDN_ROWS)
    o_l = gdn_core(mx_l, z_l, mx_c, ba, conv_w, a_log, dt_bias, norm_g, batch)
    return _finish_layer(o_l, None, xl, xc, w_out, norm2_g, up, down, mods, rows)


def sgu_layer(xl, xc, mods, rows, norm1_g, norm2_g, w_in, ln_g, ln_b, w_s, b_s, w_out, up, down,
              final_g):
    lat_row, _ = rows
    segs = [(SGU_WIDTH, BF16, _gelu_erf), (SGU_WIDTH, F32, _gelu_erf)]
    u_l, v_l = linear_pre(xl, norm1_g, mods, lat_row, w_in.astype(BF16), segs, 1024, name="sgu_in_lat")
    o_l = sgu_core(u_l, v_l, ln_g, ln_b, w_s, b_s)
    return _finish_layer(o_l, None, xl, xc, w_out, norm2_g, up, down, mods, rows, final_g=final_g)


def kernel(x, c, ctx, c_ctx, mod_w, mod_b, norm1_g, norm2_g, mlp_up, mlp_down, final_g,
           ret_w_in, ret_gn_g, ret_w_out, att_w_in, att_sink, att_w_out,
           gdn_w_in, gdn_conv_w, gdn_a_log, gdn_dt_bias, gdn_norm_g, gdn_w_out,
           sgu_w_in, sgu_ln_g, sgu_ln_b, sgu_w_s, sgu_b_s, sgu_w_out):
    batch, seq, d = x.shape
    assert (seq, d, ctx.shape[1]) == (SEQ, D_MODEL, CTX_LEN) and batch < MOD_ROWS
    xl = x.reshape(batch * SEQ, d)
    xc = ctx.reshape(batch * CTX_LEN, d)

    cond = jnp.zeros((MOD_ROWS, d), F32).at[:batch].set(c).at[batch].set(c_ctx)
    mods_all = ada_mods(cond, mod_w, mod_b).reshape(DEPTH, MOD_ROWS, 1, 6 * d)
    rows = (_row_fn(SEQ, ROW_TILE, False, batch), _row_fn(CTX_LEN, ROW_TILE, True, batch))

    xl, xc = retention_layer(xl, xc, mods_all[0], rows, batch, norm1_g[0], norm2_g[0],
                             ret_w_in[0], ret_gn_g[0], ret_w_out[0], mlp_up[0], mlp_down[0])
    xl, xc = attention_layer(xl, xc, mods_all[1], rows, batch, norm1_g[1], norm2_g[1],
                             att_w_in[0], att_sink[0], att_w_out[0], mlp_up[1], mlp_down[1])
    xl, xc = gdn_layer(xl, xc, mods_all[2], rows, batch, norm1_g[2], norm2_g[2],
                       gdn_w_in[0], gdn_conv_w[0], gdn_a_log[0], gdn_dt_bias[0], gdn_norm_g[0],
                       gdn_w_out[0], mlp_up[2], mlp_down[2])
    xl, xc = sgu_layer(xl, xc, mods_all[3], rows, norm1_g[3], norm2_g[3],
                       sgu_w_in[0], sgu_ln_g[0], sgu_ln_b[0], sgu_w_s[0], sgu_b_s[0], sgu_w_out[0],
                       mlp_up[3], mlp_down[3], final_g)
    return xl.reshape(batch, SEQ, d)
```

```python
import functools
import math

import numpy as np
import jax
import jax.numpy as jnp
from jax import lax
from jax.experimental import pallas as pl
from jax.experimental.pallas import tpu as pltpu

F32 = jnp.float32
BF16 = jnp.bfloat16

D_MODEL = 2048
SEQ = 2048
CTX_LEN = 256
GRID_W = 64
DEPTH = 4
NORM_EPS = 1e-6
ROPE_BASE = 10000.0
MLP_HIDDEN = 4 * D_MODEL

RET_HEADS = 8
RET_DK = 256
RET_DV = 512
RET_CHUNK = 128

ATT_HEADS = 16
ATT_KV_HEADS = 4
ATT_HD = 128
ATT_GROUP = 4
WINDOW = 128
ATT_BLOCK = 128

GDN_K_HEADS = 16
GDN_V_HEADS = 32
GDN_DK = 128
GDN_DV = 128
GDN_CONV = 4
GDN_CHUNK = 64

SGU_CHUNK = 128
SGU_GROUPS = 8
SGU_WIDTH = 2 * D_MODEL

VMEM_LIMIT_BYTES = 56 * 1024 * 1024
MOD_ROWS = 16
ROW_TILE = 512
NEG_BIG = -1e30


def _cparams(*sem):
    return pltpu.CompilerParams(dimension_semantics=sem, vmem_limit_bytes=VMEM_LIMIT_BYTES)


def _silu(x):
    return (0.5 * x) * (1.0 + jnp.tanh(0.5 * x))


def _identity(x):
    return x


def _gelu_erf(x):
    return 0.5 * x * (1.0 + lax.erf(x * (2.0 ** -0.5)))


def _dot(a, b):
    return jnp.dot(a, b, preferred_element_type=F32)


def _dot_nt(a, b):
    return lax.dot_general(a, b, (((1,), (1,)), ((), ())), preferred_element_type=F32)


def _dot_tn(a, b):
    return lax.dot_general(a, b, (((0,), (0,)), ((), ())), preferred_element_type=F32)


def _mods_kernel(cond_ref, w_ref, b_ref, o_ref):
    a = _silu(cond_ref[...]).astype(BF16)
    o_ref[...] = _dot(a, w_ref[...].astype(BF16)) + b_ref[...]


def ada_mods(cond, mod_w, mod_b):
    depth, d, n = mod_w.shape
    tn = 1024
    return pl.pallas_call(
        _mods_kernel,
        grid=(depth, n // tn),
        in_specs=[
            pl.BlockSpec((MOD_ROWS, d), lambda l, j: (0, 0)),
            pl.BlockSpec((None, d, tn), lambda l, j: (l, 0, j)),
            pl.BlockSpec((None, 1, tn), lambda l, j: (l, 0, j)),
        ],
        out_specs=pl.BlockSpec((None, MOD_ROWS, tn), lambda l, j: (l, 0, j)),
        out_shape=jax.ShapeDtypeStruct((depth, MOD_ROWS, n), F32),
        compiler_params=_cparams("arbitrary", "arbitrary"),
        name="ada_mods",
    )(cond, mod_w, mod_b.reshape(depth, 1, n))


def _mod_spec(row_fn, chunk, tn=None):
    if tn is None:
        return pl.BlockSpec((None, 1, D_MODEL), lambda i, j: (row_fn(i), 0, chunk))
    per = D_MODEL // tn
    return pl.BlockSpec((None, 1, tn), lambda i, j: (row_fn(i), 0, chunk * per + j))


def _row_fn(rows_per_batch, tm, is_ctx, batch):
    if is_ctx:
        return lambda i: batch
    per = rows_per_batch // tm
    return lambda i: i // per


NORM_ROWS = 16


def _norm_mod_into(x_ref, g_ref, sh_ref, sc_ref, h_ref):
    g = g_ref[...]
    shift = sh_ref[...]
    gain = 1.0 + sc_ref[...]

    def body(c, carry):
        rows = pl.ds(pl.multiple_of(c * NORM_ROWS, NORM_ROWS), NORM_ROWS)
        x = x_ref[rows, :]
        ms = jnp.mean(x * x, axis=-1, keepdims=True)
        h_ref[rows, :] = ((x * lax.rsqrt(ms + NORM_EPS) * g) * gain + shift).astype(BF16)
        return carry

    lax.fori_loop(0, x_ref.shape[0] // NORM_ROWS, body, 0, unroll=8)


def _linear_pre_kernel(x_ref, g_ref, sh_ref, sc_ref, w_ref, *rest, segs):
    out_refs = rest[:len(segs)]
    h_ref = rest[len(segs)]
    j = pl.program_id(1)

    @pl.when(j == 0)
    def _():
        _norm_mod_into(x_ref, g_ref, sh_ref, sc_ref, h_ref)

    start = 0
    for (ntiles, _, act), o_ref in zip(segs, out_refs):
        @pl.when((j >= start) & (j < start + ntiles))
        def _(o_ref=o_ref, act=act):
            o_ref[...] = act(_dot(h_ref[...], w_ref[...])).astype(o_ref.dtype)
        start += ntiles


def linear_pre(x, norm_g, mods, row_fn, w, segments, tn, tm=ROW_TILE, name="linear_pre"):
    m, d = x.shape
    n = w.shape[1]
    tm = min(tm, m)
    segs = []
    out_specs = []
    out_shapes = []
    start = 0
    for width, dtype, act in segments:
        nt = width // tn
        assert nt * tn == width
        segs.append((nt, dtype, act))
        out_specs.append(pl.BlockSpec(
            (tm, tn), lambda i, j, s=start, nt=nt: (i, jnp.clip(j - s, 0, nt - 1))))
        out_shapes.append(jax.ShapeDtypeStruct((m, width), dtype))
        start += nt
    assert start * tn == n
    outs = pl.pallas_call(
        functools.partial(_linear_pre_kernel, segs=tuple(segs)),
        grid=(m // tm, n // tn),
        in_specs=[
            pl.BlockSpec((tm, d), lambda i, j: (i, 0)),
            pl.BlockSpec((1, d), lambda i, j: (0, 0)),
            _mod_spec(row_fn, 0),
            _mod_spec(row_fn, 1),
            pl.BlockSpec((d, tn), lambda i, j: (0, j)),
        ],
        out_specs=out_specs,
        out_shape=out_shapes,
        scratch_shapes=[pltpu.VMEM((tm, d), BF16)],
        compiler_params=_cparams("arbitrary", "arbitrary"),
        name=name,
    )(x, norm_g.reshape(1, d), mods, mods, w)
    return outs


def _linear_post_kernel(a_ref, w_ref, res_ref, gate_ref, o_ref):
    o_ref[...] = res_ref[...] + gate_ref[...] * _dot(a_ref[...], w_ref[...])


def linear_post(a, w, res, mods, row_fn, gate_chunk, tn=1024, tm=ROW_TILE, name="linear_post"):
    m, k = a.shape
    n = w.shape[1]
    tm = min(tm, m)
    return pl.pallas_call(
        _linear_post_kernel,
        grid=(m // tm, n // tn),
        in_specs=[
            pl.BlockSpec((tm, k), lambda i, j: (i, 0)),
            pl.BlockSpec((k, tn), lambda i, j: (0, j)),
            pl.BlockSpec((tm, tn), lambda i, j: (i, j)),
            _mod_spec(row_fn, gate_chunk, tn),
        ],
        out_specs=pl.BlockSpec((tm, tn), lambda i, j: (i, j)),
        out_shape=jax.ShapeDtypeStruct((m, n), F32),
        compiler_params=_cparams("arbitrary", "arbitrary"),
        name=name,
    )(a, w, res, mods)


def _mlp_kernel(x_ref, g_ref, sh_ref, sc_ref, gate_ref, wu_ref, wd_ref, fg_ref, o_ref,
                h_ref, acc_ref, *, final_norm):
    j = pl.program_id(1)

    @pl.when(j == 0)
    def _():
        _norm_mod_into(x_ref, g_ref, sh_ref, sc_ref, h_ref)

    def hidden_step():
        a = jnp.maximum(_dot(h_ref[...], wu_ref[...]), 0.0)
        return _dot((a * a).astype(BF16), wd_ref[...])

    @pl.when(j == 0)
    def _():
        acc_ref[...] = hidden_step()

    @pl.when(j > 0)
    def _():
        acc_ref[...] += hidden_step()

    @pl.when(j == pl.num_programs(1) - 1)
    def _():
        y = x_ref[...] + gate_ref[...] * acc_ref[...]
        if final_norm:
            ms = jnp.mean(y * y, axis=-1, keepdims=True)
            y = y * lax.rsqrt(ms + NORM_EPS) * fg_ref[...]
        o_ref[...] = y


def mlp_block(x, norm_g, mods, row_fn, w_up, w_down, final_g=None, th=1024, tm=ROW_TILE, name="mlp"):
    m, d = x.shape
    hid = w_up.shape[1]
    tm = min(tm, m)
    final_norm = final_g is not None
    fg = (final_g if final_norm else norm_g).reshape(1, d)
    return pl.pallas_call(
        functools.partial(_mlp_kernel, final_norm=final_norm),
        grid=(m // tm, hid // th),
        in_specs=[
            pl.BlockSpec((tm, d), lambda i, j: (i, 0)),
            pl.BlockSpec((1, d), lambda i, j: (0, 0)),
            _mod_spec(row_fn, 3),
            _mod_spec(row_fn, 4),
            _mod_spec(row_fn, 5),
            pl.BlockSpec((d, th), lambda i, j: (0, j)),
            pl.BlockSpec((th, d), lambda i, j: (j, 0)),
            pl.BlockSpec((1, d), lambda i, j: (0, 0)),
        ],
        out_specs=pl.BlockSpec((tm, d), lambda i, j: (i, 0)),
        out_shape=jax.ShapeDtypeStruct((m, d), F32),
        scratch_shapes=[pltpu.VMEM((tm, d), BF16), pltpu.VMEM((tm, d), F32)],
        compiler_params=_cparams("arbitrary", "arbitrary"),
        name=name,
    )(x, norm_g.reshape(1, d), mods, mods, mods, w_up, w_down, fg)


def rope_tables(head_dim):
    rows = SEQ // GRID_W
    row = jnp.repeat(jnp.arange(rows, dtype=F32), GRID_W)
    col = jnp.tile(jnp.arange(GRID_W, dtype=F32), rows)
    axis_dim = head_dim // 2
    inv_freq = jnp.exp(-math.log(ROPE_BASE) * jnp.arange(0, axis_dim, 2, dtype=F32) / axis_dim)
    ang = jnp.concatenate([row[:, None] * inv_freq, col[:, None] * inv_freq], axis=-1)
    cos = jnp.repeat(jnp.cos(ang), 2, axis=-1)
    sin = jnp.repeat(jnp.sin(ang), 2, axis=-1)
    sign = jnp.where(jnp.arange(head_dim) % 2 == 0, -1.0, 1.0).astype(F32)
    return cos, sin * sign


def _rope(x, cos, sin_signed):
    width = x.shape[-1]
    nxt = pltpu.roll(x, width - 1, 1)
    prv = pltpu.roll(x, 1, 1)
    lane = lax.broadcasted_iota(jnp.int32, x.shape, 1)
    partner = jnp.where(lane % 2 == 0, nxt, prv)
    return x * cos + partner * sin_signed


def _ret_tables():
    lg = np.log1p(-np.exp2(-5.0 - np.arange(RET_HEADS, dtype=np.float64)))[:, None]
    pos = np.arange(RET_CHUNK, dtype=np.float64)
    dist = np.abs(pos[:, None] - pos[None, :])
    intra = np.exp(lg[:, :, None] * dist) * (1.0 + np.eye(RET_CHUNK))
    q_f = np.exp(lg * (pos + 1.0))
    q_b = np.exp(lg * (RET_CHUNK - pos))
    k_f = np.exp(lg * (RET_CHUNK - 1.0 - pos))
    k_b = np.exp(lg * pos)
    dec = np.stack([q_f, q_b, k_f, k_b], axis=1)
    dec = np.broadcast_to(dec[..., None], dec.shape + (RET_DK,))
    cdec = np.broadcast_to(np.exp(lg * RET_CHUNK)[:, :, None], (RET_HEADS, 8, RET_DV))
    return (jnp.asarray(intra, F32), jnp.asarray(dec, F32), jnp.asarray(cdec, F32))


def _ret_kernel(ql_ref, kl_ref, vl_ref, gl_ref, qc_ref, kc_ref, vc_ref, gc_ref,
                cos_ref, sin_ref, intra_ref, dec_ref, cdec_ref, gn_ref,
                ol_ref, oc_ref, qs_ref, ks_ref, sb_ref, st_ref):
    C = RET_CHUNK
    n_lat = SEQ // C
    n_ctx = CTX_LEN // C
    kscale = RET_DK ** -0.5
    cd = cdec_ref[0:1, :]

    def rope_chunk(c, carry):
        rows = pl.ds(pl.multiple_of(c * C, C), C)
        cs = cos_ref[rows, :]
        sn = sin_ref[rows, :]
        qs_ref[rows, :] = _rope(ql_ref[rows, :], cs, sn)
        ks_ref[rows, :] = _rope(kl_ref[rows, :], cs, sn) * kscale
        return carry

    lax.fori_loop(0, n_lat, rope_chunk, 0)

    st_ref[...] = jnp.zeros_like(st_ref)

    def back_step(k, v, slot):
        sb_ref[slot] = st_ref[...].astype(BF16)
        kd = (k * dec_ref[3]).astype(BF16)
        st_ref[...] = st_ref[...] * cd + _dot_tn(kd, v)

    for c in reversed(range(n_ctx)):
        rows = pl.ds(c * C, C)
        back_step(kc_ref[rows, :] * kscale, vc_ref[rows, :], c)

    def back_lat(t, carry):
        c = n_lat - 1 - t
        rows = pl.ds(pl.multiple_of(c * C, C), C)
        back_step(ks_ref[rows, :], vl_ref[rows, :], n_ctx + c)
        return carry

    lax.fori_loop(0, n_lat, back_lat, 0)

    st_ref[...] = jnp.zeros_like(st_ref)

    def fwd_step(q, k, v, gate, slot):
        s = _dot_nt(q.astype(BF16), k.astype(BF16)) * intra_ref[...]
        st = st_ref[...]
        o = (_dot(s.astype(BF16), v)
             + _dot((q * dec_ref[0]).astype(BF16), st.astype(BF16))
             + _dot((q * dec_ref[1]).astype(BF16), sb_ref[slot]))
        st_ref[...] = st * cd + _dot_tn((k * dec_ref[2]).astype(BF16), v)
        ms = jnp.mean(o * o, axis=-1, keepdims=True)
        on = o * lax.rsqrt(ms + NORM_EPS) * gn_ref[...]
        return (on * gate.astype(F32)).astype(BF16)

    for c in range(n_ctx):
        rows = pl.ds(c * C, C)
        oc_ref[rows, :] = fwd_step(qc_ref[rows, :], kc_ref[rows, :] * kscale, vc_ref[rows, :],
                                   gc_ref[rows, :], c)

    def fwd_lat(c, carry):
        rows = pl.ds(pl.multiple_of(c * C, C), C)
        ol_ref[rows, :] = fwd_step(qs_ref[rows, :], ks_ref[rows, :], vl_ref[rows, :],
                                   gl_ref[rows, :], n_ctx + c)
        return carry

    lax.fori_loop(0, n_lat, fwd_lat, 0)


def retention_core(qk_lat, v_lat, g_lat, qk_ctx, v_ctx, g_ctx, gn_g, batch):
    H = RET_HEADS
    cos, sin = rope_tables(RET_DK)
    intra, dec, cdec = _ret_tables()
    n_chunks = (SEQ + CTX_LEN) // RET_CHUNK

    def specs(rows):
        return [
            pl.BlockSpec((rows, RET_DK), lambda b, h: (b, h)),
            pl.BlockSpec((rows, RET_DK), lambda b, h: (b, H + h)),
            pl.BlockSpec((rows, RET_DV), lambda b, h: (b, h)),
            pl.BlockSpec((rows, RET_DV), lambda b, h: (b, h)),
        ]

    return pl.pallas_call(
        _ret_kernel,
        grid=(batch, H),
        in_specs=specs(SEQ) + specs(CTX_LEN) + [
            pl.BlockSpec((SEQ, RET_DK), lambda b, h: (0, 0)),
            pl.BlockSpec((SEQ, RET_DK), lambda b, h: (0, 0)),
            pl.BlockSpec((None, RET_CHUNK, RET_CHUNK), lambda b, h: (h, 0, 0)),
            pl.BlockSpec((None, 4, RET_CHUNK, RET_DK), lambda b, h: (h, 0, 0, 0)),
            pl.BlockSpec((None, 8, RET_DV), lambda b, h: (h, 0, 0)),
            pl.BlockSpec((1, RET_DV), lambda b, h: (0, h)),
        ],
        out_specs=[
            pl.BlockSpec((SEQ, RET_DV), lambda b, h: (b, h)),
            pl.BlockSpec((CTX_LEN, RET_DV), lambda b, h: (b, h)),
        ],
        out_shape=[
            jax.ShapeDtypeStruct((batch * SEQ, H * RET_DV), BF16),
            jax.ShapeDtypeStruct((batch * CTX_LEN, H * RET_DV), BF16),
        ],
        scratch_shapes=[
            pltpu.VMEM((SEQ, RET_DK), F32),
            pltpu.VMEM((SEQ, RET_DK), F32),
            pltpu.VMEM((n_chunks, RET_DK, RET_DV), BF16),
            pltpu.VMEM((RET_DK, RET_DV), F32),
        ],
        compiler_params=_cparams("arbitrary", "arbitrary"),
        name="retention_core",
    )(qk_lat, qk_lat, v_lat, g_lat, qk_ctx, qk_ctx, v_ctx, g_ctx,
      cos, sin, intra, dec, cdec, gn_g.reshape(1, H * RET_DV))


def _att_kernel(sink_ref, ql_ref, kl_ref, vl_ref, qc_ref, kc_ref, vc_ref, cos_ref, sin_ref,
                ol_ref, oc_ref, kp_ref, vp_ref):
    BLK = ATT_BLOCK
    G = ATT_GROUP
    HD = ATT_HD
    n_blk = SEQ // BLK
    span = BLK + 2 * WINDOW
    scale = HD ** -0.5
    kvh = pl.program_id(1)

    zpad = jnp.zeros((WINDOW, HD), BF16)
    kp_ref[pl.ds(0, WINDOW), :] = zpad
    vp_ref[pl.ds(0, WINDOW), :] = zpad
    kp_ref[pl.ds(WINDOW + SEQ, WINDOW), :] = zpad
    vp_ref[pl.ds(WINDOW + SEQ, WINDOW), :] = zpad
    ctx0 = SEQ + 2 * WINDOW
    kp_ref[pl.ds(ctx0, CTX_LEN), :] = kc_ref[...].astype(BF16)
    vp_ref[pl.ds(ctx0, CTX_LEN), :] = vc_ref[...]

    def stage(c, carry):
        rows = pl.ds(pl.multiple_of(c * BLK, BLK), BLK)
        dst = pl.ds(pl.multiple_of(c * BLK + WINDOW, BLK), BLK)
        kp_ref[dst, :] = _rope(kl_ref[rows, :], cos_ref[rows, :], sin_ref[rows, :]).astype(BF16)
        vp_ref[dst, :] = vl_ref[rows, :]
        return carry

    lax.fori_loop(0, n_blk, stage, 0)

    ridx = lax.broadcasted_iota(jnp.int32, (G * BLK, 1), 0)
    grp = ridx // BLK
    sink_col = jnp.zeros((G * BLK, 1), F32)
    for g in range(G):
        sink_col = jnp.where(grp == g, sink_ref[kvh * G + g], sink_col)
    kc_b = kp_ref[pl.ds(ctx0, CTX_LEN), :]
    vc_b = vp_ref[pl.ds(ctx0, CTX_LEN), :]

    def finish(parts, out_ref, rows):
        m = sink_col
        for s, _ in parts:
            m = jnp.maximum(m, jnp.max(s, axis=-1, keepdims=True))
        den = jnp.exp(sink_col - m)
        acc = None
        for s, v in parts:
            p = jnp.exp(s - m)
            den = den + jnp.sum(p, axis=-1, keepdims=True)
            pv = _dot(p.astype(BF16), v)
            acc = pv if acc is None else acc + pv
        o = acc / den
        for g in range(G):
            out_ref[rows, pl.ds(g * HD, HD)] = o[g * BLK:(g + 1) * BLK, :].astype(BF16)

    qi = ridx % BLK
    col = lax.broadcasted_iota(jnp.int32, (G * BLK, span), 1)
    band = (col >= qi) & (col <= qi + 2 * WINDOW)

    def lat_block(bi, carry):
        rows = pl.ds(pl.multiple_of(bi * BLK, BLK), BLK)
        cs = cos_ref[rows, :]
        sn = sin_ref[rows, :]
        qs = [(_rope(ql_ref[rows, pl.ds(g * HD, HD)], cs, sn) * scale).astype(BF16) for g in range(G)]
        q4 = jnp.concatenate(qs, axis=0)
        krows = pl.ds(pl.multiple_of(bi * BLK, BLK), span)
        s_lat = _dot_nt(q4, kp_ref[krows, :])
        kpos = col + (bi * BLK - WINDOW)
        valid = band & (kpos >= 0) & (kpos < SEQ)
        s_lat = jnp.where(valid, s_lat, NEG_BIG)
        s_ctx = _dot_nt(q4, kc_b)
        finish([(s_lat, vp_ref[krows, :]), (s_ctx, vc_b)], ol_ref, rows)
        return carry

    lax.fori_loop(0, n_blk, lat_block, 0)

    for bi in range(CTX_LEN // BLK):
        rows = pl.ds(bi * BLK, BLK)
        qs = [(qc_ref[rows, pl.ds(g * HD, HD)] * scale).astype(BF16) for g in range(G)]
        q4 = jnp.concatenate(qs, axis=0)
        finish([(_dot_nt(q4, kc_b), vc_b)], oc_ref, rows)


def attention_core(q_lat, k_lat, v_lat, q_ctx, k_ctx, v_ctx, sink, batch):
    cos, sin = rope_tables(ATT_HD)
    GW = ATT_GROUP * ATT_HD
    pad_rows = SEQ + 2 * WINDOW + CTX_LEN

    def specs(rows):
        return [
            pl.BlockSpec((rows, GW), lambda b, h: (b, h)),
            pl.BlockSpec((rows, ATT_HD), lambda b, h: (b, h)),
            pl.BlockSpec((rows, ATT_HD), lambda b, h: (b, h)),
        ]

    return pl.pallas_call(
        _att_kernel,
        grid=(batch, ATT_KV_HEADS),
        in_specs=[pl.BlockSpec(memory_space=pltpu.SMEM)] + specs(SEQ) + specs(CTX_LEN) + [
            pl.BlockSpec((SEQ, ATT_HD), lambda b, h: (0, 0)),
            pl.BlockSpec((SEQ, ATT_HD), lambda b, h: (0, 0)),
        ],
        out_specs=[
            pl.BlockSpec((SEQ, GW), lambda b, h: (b, h)),
            pl.BlockSpec((CTX_LEN, GW), lambda b, h: (b, h)),
        ],
        out_shape=[
            jax.ShapeDtypeStruct((batch * SEQ, ATT_HEADS * ATT_HD), BF16),
            jax.ShapeDtypeStruct((batch * CTX_LEN, ATT_HEADS * ATT_HD), BF16),
        ],
        scratch_shapes=[
            pltpu.VMEM((pad_rows, ATT_HD), BF16),
            pltpu.VMEM((pad_rows, ATT_HD), BF16),
        ],
        compiler_params=_cparams("arbitrary", "arbitrary"),
        name="attention_core",
    )(sink.astype(F32), q_lat, k_lat, v_lat, q_ctx, k_ctx, v_ctx, cos, sin)


def _sgu_kernel(u_ref, v_ref, g_ref, b_ref, ws_ref, bs_ref, o_ref, *, chunks):
    C = SGU_CHUNK
    gw = SGU_WIDTH // SGU_GROUPS
    for c in range(chunks):
        rows = pl.ds(c * C, C)
        v = v_ref[rows, :]
        mu = jnp.mean(v, axis=-1, keepdims=True)
        xc = v - mu
        var = jnp.mean(xc * xc, axis=-1, keepdims=True)
        vn = (xc * lax.rsqrt(var + NORM_EPS) * g_ref[...] + b_ref[...]).astype(BF16)
        for g in range(SGU_GROUPS):
            cols = pl.ds(g * gw, gw)
            t = _dot(ws_ref[g], vn[:, g * gw:(g + 1) * gw]) + bs_ref[g]
            o_ref[rows, cols] = (u_ref[rows, cols].astype(F32) * t).astype(BF16)


def sgu_core(u, v, ln_g, ln_b, w_s, b_s, chunks=2):
    m = u.shape[0]
    tm = chunks * SGU_CHUNK
    return pl.pallas_call(
        functools.partial(_sgu_kernel, chunks=chunks),
        grid=(m // tm,),
        in_specs=[
            pl.BlockSpec((tm, SGU_WIDTH), lambda i: (i, 0)),
            pl.BlockSpec((tm, SGU_WIDTH), lambda i: (i, 0)),
            pl.BlockSpec((1, SGU_WIDTH), lambda i: (0, 0)),
            pl.BlockSpec((1, SGU_WIDTH), lambda i: (0, 0)),
            pl.BlockSpec((SGU_GROUPS, SGU_CHUNK, SGU_CHUNK), lambda i: (0, 0, 0)),
            pl.BlockSpec((SGU_GROUPS, SGU_CHUNK, 1), lambda i: (0, 0, 0)),
        ],
        out_specs=pl.BlockSpec((tm, SGU_WIDTH), lambda i: (i, 0)),
        out_shape=jax.ShapeDtypeStruct((m, SGU_WIDTH), BF16),
        compiler_params=_cparams("arbitrary"),
        name="sgu_core",
    )(u, v, ln_g.reshape(1, -1), ln_b.reshape(1, -1), w_s.astype(BF16),
      b_s.reshape(SGU_GROUPS, SGU_CHUNK, 1))


GDN_ROWS = CTX_LEN + SEQ
GDN_NCHUNK = GDN_ROWS // GDN_CHUNK
GDN_CONV_ROWS = 256
GDN_HALO = 8
GDN_PREP_TILES = 3


def _gdn_kernel(ql_ref, kl_ref, vl_ref, zl_ref, qc_ref, kc_ref, vc_ref, ba_ref,
                cwq_ref, cwk_ref, cwv_ref, alog_ref, dt_ref, ng_ref,
                ol_ref,
                xp_ref, qn_ref, kn_ref, vv_ref, gt_ref, qp_ref, g_ref, b_ref,
                el_ref, of_ref, ob_ref, s_ref):
    C = GDN_CHUNK
    DK = GDN_DK
    R = GDN_CONV_ROWS
    HALO = GDN_HALO

    def conv_silu(src_ref, n_rows, cw_ref, width, store):
        cols = pl.ds(0, width)
        xp_ref[pl.ds(0, HALO), cols] = jnp.zeros((HALO, width), F32)
        xp_ref[pl.ds(HALO, n_rows), cols] = src_ref[...]
        xp_ref[pl.ds(HALO + n_rows, HALO), cols] = jnp.zeros((HALO, width), F32)
        w = cw_ref[...]
        n = R + 2 * HALO

        def body(c, carry):
            r0 = pl.multiple_of(c * R, R)
            x = xp_ref[pl.ds(r0, n), cols]
            y = (w[2:3] * x + w[1:2] * pltpu.roll(x, 1, 0) + w[0:1] * pltpu.roll(x, 2, 0)
                 + w[3:4] * pltpu.roll(x, n - 1, 0))[HALO:HALO + R]
            store(r0, _silu(y))
            return carry

        lax.fori_loop(0, n_rows // R, body, 0)

    def unit(y):
        return y * lax.rsqrt(jnp.sum(y * y, axis=-1, keepdims=True) + 1e-6)

    def store_q(off):
        def f(r0, y):
            qn_ref[pl.ds(pl.multiple_of(off + r0, R), R), :] = unit(y) * (DK ** -0.5)
        return f

    def store_k(off):
        def f(r0, y):
            kn_ref[pl.ds(pl.multiple_of(off + r0, R), R), :] = unit(y)
        return f

    def store_v(off):
        def f(r0, y):
            vv_ref[pl.ds(pl.multiple_of(off + r0, R), R), :] = y
        return f

    conv_silu(qc_ref, CTX_LEN, cwq_ref, DK, store_q(0))
    conv_silu(kc_ref, CTX_LEN, cwk_ref, DK, store_k(0))
    conv_silu(vc_ref, CTX_LEN, cwv_ref, 2 * GDN_DV, store_v(0))
    conv_silu(ql_ref, SEQ, cwq_ref, DK, store_q(CTX_LEN))
    conv_silu(kl_ref, SEQ, cwk_ref, DK, store_k(CTX_LEN))
    conv_silu(vl_ref, SEQ, cwv_ref, 2 * GDN_DV, store_v(CTX_LEN))

    raw = ba_ref[...]
    xg = raw + dt_ref[...]
    softplus = jnp.maximum(xg, 0.0) + jnp.log1p(jnp.exp(-jnp.abs(xg)))
    g = -jnp.exp(alog_ref[...]) * softplus
    pos = lax.broadcasted_iota(jnp.int32, raw.shape, 1) % C
    pre = g
    suf = g
    shift = 1
    while shift < C:
        pre = pre + jnp.where(pos >= shift, pltpu.roll(pre, shift, 1), 0.0)
        suf = suf + jnp.where(pos + shift < C, pltpu.roll(suf, GDN_ROWS - shift, 1), 0.0)
        shift *= 2
    row = lax.broadcasted_iota(jnp.int32, raw.shape, 0)
    gt_ref[...] = jnp.where(row < 4, jax.nn.sigmoid(raw), jnp.where(row < 6, pre, suf))

    PK = 2 * C
    ri = lax.broadcasted_iota(jnp.int32, (C, PK), 0)
    li = lax.broadcasted_iota(jnp.int32, (C, PK), 1)
    ci = li % C
    left = li < C
    eye = (ri == ci).astype(F32)
    incl = (ri >= ci, ri <= ci)
    strict = (ri > ci, ri < ci)
    half_sel = (li == ri, li == ri + C)

    def to_col(rowvec, half):
        return jnp.sum(jnp.where(half_sel[half], rowvec, 0.0), axis=1, keepdims=True)

    def block_diag(m, zero):
        return jnp.concatenate([jnp.concatenate([m[0], zero], axis=1),
                                jnp.concatenate([zero, m[1]], axis=1)], axis=0)

    def dot3_packed(x, m):
        bd = jnp.concatenate([jnp.where(left, m, 0.0), jnp.where(left, 0.0, m)], axis=0)
        x_hi = x.astype(BF16)
        bd_hi = bd.astype(BF16)
        x_lo = (x - x_hi.astype(F32)).astype(BF16)
        bd_lo = (bd - bd_hi.astype(F32)).astype(BF16)
        rhs = jnp.concatenate([jnp.concatenate([bd_hi, bd_lo], axis=1),
                               jnp.concatenate([bd_hi, jnp.zeros_like(bd_hi)], axis=1)], axis=0)
        out = _dot(jnp.concatenate([x_hi, x_lo], axis=1), rhs)
        return out[:, :PK] + out[:, PK:]

    def prep_pair(p, carry):
        zero_uw = jnp.zeros((C, 2 * GDN_DV), BF16)
        chains = []
        for sub in range(2 * GDN_PREP_TILES):
            tile, half = divmod(sub, 2)
            if half == 0:
                lanes = pl.ds(pl.multiple_of((p * GDN_PREP_TILES + tile) * PK, PK), PK)
                gates = gt_ref[:, lanes]
                gates_sw = pltpu.roll(gates, C, 1)
            chunk = (p * GDN_PREP_TILES + tile) * 2 + half
            rows = pl.ds(pl.multiple_of(chunk * C, C), C)
            q = qn_ref[rows, :]
            k = kn_ref[rows, :]
            v2 = vv_ref[rows, :]
            qk_kk = _dot_nt(jnp.concatenate([q, k], axis=0).astype(BF16),
                            jnp.concatenate([k, k], axis=0).astype(BF16))
            for d in range(2):
                def packed_row(r, half=half, d=d):
                    a = (gates if half == 0 else gates_sw)[r + 2 * d:r + 2 * d + 1]
                    b = (gates_sw if half == 0 else gates)[r + 2 * d + 1:r + 2 * d + 2]
                    return jnp.where(left[0:1], a, b)
                edge = half * C + (C - 1 if d == 0 else 0)
                chains.append(dict(
                    d=d, n=chunk, rows=rows, q=q, k=k, v2=v2, qk=qk_kk[:C], kk=qk_kk[C:],
                    grow=packed_row(4),
                    bcols=[to_col(gates[2 * d + j:2 * d + j + 1], half) for j in range(2)],
                    gcols=[to_col(gates[4 + 2 * d + j:5 + 2 * d + j], half) for j in range(2)],
                    glast=[gates[4 + 2 * d + j:5 + 2 * d + j, edge:edge + 1] for j in range(2)]))
        for ch in chains:
            d = ch["d"]
            bcol = jnp.where(left, ch["bcols"][0], ch["bcols"][1])
            gcol = jnp.where(left, ch["gcols"][0], ch["gcols"][1])
            ch["dec"] = jnp.exp(jnp.where(incl[d], gcol - ch["grow"], NEG_BIG))
            nmat = jnp.where(strict[d], -(ch["kk"] * bcol * ch["dec"]), 0.0)
            ch["pm"] = eye + nmat
            ch["nmat"] = nmat
        for ch in chains:
            ch["mm"] = dot3_packed(ch["nmat"], ch["nmat"])
        for _ in range(4):
            for ch in chains:
                r = dot3_packed(jnp.concatenate([ch["pm"], ch["mm"]], axis=0), ch["mm"])
                ch["pm"] = ch["pm"] + r[:C]
                ch["mm"] = r[C:]
        for ch in chains:
            ch["r"] = dot3_packed(ch["pm"], ch["mm"])
        for ch in chains:
            tmat = (ch["pm"] + ch["r"]).astype(BF16)
            ch["eg"] = [jnp.exp(g) for g in ch["gcols"]]
            rhs = [jnp.concatenate([ch["v2"][:, j * GDN_DV:(j + 1) * GDN_DV] * ch["bcols"][j],
                                    ch["k"] * (ch["bcols"][j] * ch["eg"][j])], axis=1).astype(BF16)
                   for j in range(2)]
            ch["uw"] = _dot(tmat, block_diag(rhs, zero_uw)).astype(BF16)
        for ch in chains:
            uw = [ch["uw"][:, j * 2 * GDN_DV:(j + 1) * 2 * GDN_DV] for j in range(2)]
            amat = (ch["qk"] * ch["dec"]).astype(BF16)
            ch["a_uw"] = _dot(amat, block_diag(uw, zero_uw))
            ch["k_uw"] = [_dot_tn((ch["k"] * jnp.exp(ch["glast"][j] - ch["gcols"][j])).astype(BF16), uw[j])
                          for j in range(2)]
        for ch in chains:
            n, rows = ch["n"], ch["rows"]
            out_ref = of_ref if ch["d"] == 0 else ob_ref
            for j in range(2):
                combo = 2 * ch["d"] + j
                a_u = ch["a_uw"][:, (2 * j) * GDN_DV:(2 * j + 1) * GDN_DV]
                a_w = ch["a_uw"][:, (2 * j + 1) * GDN_DV:(2 * j + 2) * GDN_DV]
                qp_ref[combo, rows, :] = (ch["q"] * ch["eg"][j] - a_w).astype(BF16)
                out_ref[rows, pl.ds(j * GDN_DV, GDN_DV)] = a_u
                b_ref[combo, n] = ch["k_uw"][j][:, :GDN_DV]
                g_ref[combo, n] = ch["k_uw"][j][:, GDN_DV:].astype(BF16)
                el_ref[combo, n] = jnp.broadcast_to(jnp.exp(ch["glast"][j]), (8, GDN_DV))
        return carry

    lax.fori_loop(0, GDN_NCHUNK // (2 * GDN_PREP_TILES), prep_pair, 0)

    s_ref[...] = jnp.zeros_like(s_ref)

    def rec_pair(nf, nb):
        work = []
        for combo in range(4):
            n = nf if combo < 2 else nb
            rows = pl.ds(pl.multiple_of(n * C, C), C)
            out_ref = of_ref if combo < 2 else ob_ref
            cols = pl.ds((combo % 2) * GDN_DV, GDN_DV)
            lhs = jnp.concatenate([g_ref[combo, n], qp_ref[combo, rows, :]], axis=0)
            work.append((combo, out_ref, rows, cols, lhs, s_ref[combo], b_ref[combo, n],
                         el_ref[combo, n][0:1, :], out_ref[rows, cols]))
        done = []
        for combo, out_ref, rows, cols, lhs, st, bmat, e_last, o0 in work:
            r = _dot(lhs, st.astype(BF16))
            done.append((combo, out_ref, rows, cols, st * e_last + bmat - r[:GDN_DK], o0 + r[GDN_DK:]))
        for combo, out_ref, rows, cols, st_new, o in done:
            s_ref[combo] = st_new
            out_ref[rows, cols] = o

    n_ctx = CTX_LEN // C
    for t in range(n_ctx):
        rec_pair(t, n_ctx - 1 - t)

    def rec_lat(t, carry):
        rec_pair(n_ctx + t, GDN_NCHUNK - 1 - t)
        return carry

    lax.fori_loop(0, SEQ // C, rec_lat, 0)

    def fin(c, carry):
        src = pl.ds(pl.multiple_of(CTX_LEN + c * R, R), R)
        dst = pl.ds(pl.multiple_of(c * R, R), R)
        for j in range(2):
            cols = pl.ds(j * GDN_DV, GDN_DV)
            o = of_ref[src, cols] + ob_ref[src, cols]
            ms = jnp.mean(o * o, axis=-1, keepdims=True)
            on = o * lax.rsqrt(ms + NORM_EPS) * ng_ref[...]
            ol_ref[dst, cols] = (on * zl_ref[dst, cols].astype(F32)).astype(BF16)
        return carry

    lax.fori_loop(0, SEQ // R, fin, 0)


def gdn_core(mixed_lat, z_lat, mixed_ctx, ba, conv_w, a_log, dt_bias, norm_g, batch):
    HK = GDN_K_HEADS
    DK = GDN_DK
    VW = 2 * GDN_DV
    v_blk0 = (2 * HK * DK) // VW

    def tab(t):
        t = t.reshape(2, HK, 2).transpose(1, 0, 2).reshape(HK, 4)
        return jnp.concatenate([jnp.zeros_like(t), t], axis=1).reshape(HK, 8, 1).astype(F32)

    def specs(rows):
        return [
            pl.BlockSpec((rows, DK), lambda b, h: (b, h)),
            pl.BlockSpec((rows, DK), lambda b, h: (b, HK + h)),
            pl.BlockSpec((rows, VW), lambda b, h: (b, v_blk0 + h)),
        ]

    nc = GDN_NCHUNK
    return pl.pallas_call(
        _gdn_kernel,
        grid=(batch, HK),
        in_specs=specs(SEQ) + [pl.BlockSpec((SEQ, VW), lambda b, h: (b, h))] + specs(CTX_LEN) + [
            pl.BlockSpec((None, None, 8, GDN_ROWS), lambda b, h: (b, h, 0, 0)),
            pl.BlockSpec((GDN_CONV, DK), lambda b, h: (0, h)),
            pl.BlockSpec((GDN_CONV, DK), lambda b, h: (0, HK + h)),
            pl.BlockSpec((GDN_CONV, VW), lambda b, h: (0, v_blk0 + h)),
            pl.BlockSpec((None, 8, 1), lambda b, h: (h, 0, 0)),
            pl.BlockSpec((None, 8, 1), lambda b, h: (h, 0, 0)),
            pl.BlockSpec((1, GDN_DV), lambda b, h: (0, 0)),
        ],
        out_specs=pl.BlockSpec((SEQ, VW), lambda b, h: (b, h)),
        out_shape=jax.ShapeDtypeStruct((batch * SEQ, GDN_V_HEADS * GDN_DV), BF16),
        scratch_shapes=[
            pltpu.VMEM((SEQ + 2 * GDN_HALO, VW), F32),
            pltpu.VMEM((GDN_ROWS, DK), F32),
            pltpu.VMEM((GDN_ROWS, DK), F32),
            pltpu.VMEM((GDN_ROWS, VW), F32),
            pltpu.VMEM((8, GDN_ROWS), F32),
            pltpu.VMEM((4, GDN_ROWS, DK), BF16),
            pltpu.VMEM((4, nc, DK, GDN_DV), BF16),
            pltpu.VMEM((4, nc, DK, GDN_DV), F32),
            pltpu.VMEM((4, nc, 8, GDN_DV), F32),
            pltpu.VMEM((GDN_ROWS, VW), F32),
            pltpu.VMEM((GDN_ROWS, VW), F32),
            pltpu.VMEM((4, DK, GDN_DV), F32),
        ],
        compiler_params=_cparams("arbitrary", "arbitrary"),
        name="gdn_core",
    )(mixed_lat, mixed_lat, mixed_lat, z_lat, mixed_ctx, mixed_ctx, mixed_ctx, ba,
      conv_w, conv_w, conv_w, tab(a_log), tab(dt_bias), norm_g.reshape(1, GDN_DV))


def _finish_layer(o_lat, o_ctx, xl, xc, w_out, norm2_g, up, down, mods, rows, final_g=None):
    lat_row, ctx_row = rows
    w_out = w_out.astype(BF16)
    up = up.astype(BF16)
    down = down.astype(BF16)
    xl = linear_post(o_lat, w_out, xl, mods, lat_row, 2, name="out_proj_lat")
    xl = mlp_block(xl, norm2_g, mods, lat_row, up, down, final_g=final_g, name="mlp_lat")
    if o_ctx is not None:
        xc = linear_post(o_ctx, w_out, xc, mods, ctx_row, 2, name="out_proj_ctx")
        xc = mlp_block(xc, norm2_g, mods, ctx_row, up, down, name="mlp_ctx")
    return xl, xc


def retention_layer(xl, xc, mods, rows, batch, norm1_g, norm2_g, w_in, gn_g, w_out, up, down):
    lat_row, ctx_row = rows
    w_in = w_in.astype(BF16)
    qkw = 2 * RET_HEADS * RET_DK
    vw = RET_HEADS * RET_DV
    segs = [(qkw, F32, _identity), (vw, BF16, _identity), (vw, BF16, _silu)]
    qk_l, v_l, g_l = linear_pre(xl, norm1_g, mods, lat_row, w_in, segs, 1024, name="ret_in_lat")
    qk_c, v_c, g_c = linear_pre(xc, norm1_g, mods, ctx_row, w_in, segs, 1024, name="ret_in_ctx")
    o_l, o_c = retention_core(qk_l, v_l, g_l, qk_c, v_c, g_c, gn_g, batch)
    return _finish_layer(o_l, o_c, xl, xc, w_out, norm2_g, up, down, mods, rows)


def attention_layer(xl, xc, mods, rows, batch, norm1_g, norm2_g, w_in, sink, w_out, up, down):
    lat_row, ctx_row = rows
    w_in = w_in.astype(BF16)
    qw = ATT_HEADS * ATT_HD
    kvw = ATT_KV_HEADS * ATT_HD
    segs = [(qw, F32, _identity), (kvw, F32, _identity), (kvw, BF16, _identity)]
    q_l, k_l, v_l = linear_pre(xl, norm1_g, mods, lat_row, w_in, segs, 512, name="att_in_lat")
    q_c, k_c, v_c = linear_pre(xc, norm1_g, mods, ctx_row, w_in, segs, 512, name="att_in_ctx")
    o_l, o_c = attention_core(q_l, k_l, v_l, q_c, k_c, v_c, sink, batch)
    return _finish_layer(o_l, o_c, xl, xc, w_out, norm2_g, up, down, mods, rows)


def gdn_layer(xl, xc, mods, rows, batch, norm1_g, norm2_g, w_in, conv_w, a_log, dt_bias, norm_g,
              w_out, up, down):
    lat_row, ctx_row = rows
    conv_ch = 2 * GDN_K_HEADS * GDN_DK + GDN_V_HEADS * GDN_DV
    zw = GDN_V_HEADS * GDN_DV
    w_main = w_in[:, :conv_ch + zw].astype(BF16)
    w_gate = w_in[:, conv_ch + zw:].astype(BF16)
    segs = [(conv_ch, F32, _identity), (zw, BF16, _silu)]
    gsegs = [(4 * GDN_V_HEADS, F32, _identity)]
    mx_l, z_l = linear_pre(xl, norm1_g, mods, lat_row, w_main, segs, 1024, name="gdn_in_lat")
    mx_c, _ = linear_pre(xc, norm1_g, mods, ctx_row, w_main, segs, 1024, name="gdn_in_ctx")
    (ba_l,) = linear_pre(xl, norm1_g, mods, lat_row, w_gate, gsegs, 128, name="gdn_gate_lat")
    (ba_c,) = linear_pre(xc, norm1_g, mods, ctx_row, w_gate, gsegs, 128, name="gdn_gate_ctx")
    ba = jnp.concatenate([ba_c.reshape(batch, CTX_LEN, -1), ba_l.reshape(batch, SEQ, -1)], axis=1)
    ba = ba.reshape(batch, GDN_ROWS, 2, 2, GDN_K_HEADS, 2).transpose(0, 4, 2, 3, 5, 1)
    ba = ba.reshape(batch, GDN_K_HEADS, 8, GDN_ROWS)
    o_l = gdn_core(mx_l, z_l, mx_c, ba, conv_w, a_log, dt_bias, norm_g, batch)
    return _finish_layer(o_l, None, xl, xc, w_out, norm2_g, up, down, mods, rows)


def sgu_layer(xl, xc, mods, rows, norm1_g, norm2_g, w_in, ln_g, ln_b, w_s, b_s, w_out, up, down,
              final_g):
    lat_row, _ = rows
    segs = [(SGU_WIDTH, BF16, _gelu_erf), (SGU_WIDTH, F32, _gelu_erf)]
    u_l, v_l = linear_pre(xl, norm1_g, mods, lat_row, w_in.astype(BF16), segs, 1024, name="sgu_in_lat")
    o_l = sgu_core(u_l, v_l, ln_g, ln_b, w_s, b_s)
    return _finish_layer(o_l, None, xl, xc, w_out, norm2_g, up, down, mods, rows, final_g=final_g)


def kernel(x, c, ctx, c_ctx, mod_w, mod_b, norm1_g, norm2_g, mlp_up, mlp_down, final_g,
           ret_w_in, ret_gn_g, ret_w_out, att_w_in, att_sink, att_w_out,
           gdn_w_in, gdn_conv_w, gdn_a_log, gdn_dt_bias, gdn_norm_g, gdn_w_out,
           sgu_w_in, sgu_ln_g, sgu_ln_b, sgu_w_s, sgu_b_s, sgu_w_out):
    batch, seq, d = x.shape
    assert (seq, d, ctx.shape[1]) == (SEQ, D_MODEL, CTX_LEN) and batch < MOD_ROWS
    xl = x.reshape(batch * SEQ, d)
    xc = ctx.reshape(batch * CTX_LEN, d)

    cond = jnp.zeros((MOD_ROWS, d), F32).at[:batch].set(c).at[batch].set(c_ctx)
    mods_all = ada_mods(cond, mod_w, mod_b).reshape(DEPTH, MOD_ROWS, 1, 6 * d)
    rows = (_row_fn(SEQ, ROW_TILE, False, batch), _row_fn(CTX_LEN, ROW_TILE, True, batch))

    xl, xc = retention_layer(xl, xc, mods_all[0], rows, batch, norm1_g[0], norm2_g[0],
                             ret_w_in[0], ret_gn_g[0], ret_w_out[0], mlp_up[0], mlp_down[0])
    xl, xc = attention_layer(xl, xc, mods_all[1], rows, batch, norm1_g[1], norm2_g[1],
                             att_w_in[0], att_sink[0], att_w_out[0], mlp_up[1], mlp_down[1])
    xl, xc = gdn_layer(xl, xc, mods_all[2], rows, batch, norm1_g[2], norm2_g[2],
                       gdn_w_in[0], gdn_conv_w[0], gdn_a_log[0], gdn_dt_bias[0], gdn_norm_g[0],
                       gdn_w_out[0], mlp_up[2], mlp_down[2])
    xl, xc = sgu_layer(xl, xc, mods_all[3], rows, norm1_g[3], norm2_g[3],
                       sgu_w_in[0], sgu_ln_g[0], sgu_ln_b[0], sgu_w_s[0], sgu_b_s[0], sgu_w_out[0],
                       mlp_up[3], mlp_down[3], final_g)
    return xl.reshape(batch, SEQ, d)
```

```python
import functools
import math

import numpy as np
import jax
import jax.numpy as jnp
from jax import lax
from jax.experimental import pallas as pl
from jax.experimental.pallas import tpu as pltpu

F32 = jnp.float32
BF16 = jnp.bfloat16

D_MODEL = 2048
SEQ = 2048
CTX_LEN = 256
GRID_W = 64
DEPTH = 4
NORM_EPS = 1e-6
ROPE_BASE = 10000.0
MLP_HIDDEN = 4 * D_MODEL

RET_HEADS = 8
RET_DK = 256
RET_DV = 512
RET_CHUNK = 128

ATT_HEADS = 16
ATT_KV_HEADS = 4
ATT_HD = 128
ATT_GROUP = 4
WINDOW = 128
ATT_BLOCK = 128

GDN_K_HEADS = 16
GDN_V_HEADS = 32
GDN_DK = 128
GDN_DV = 128
GDN_CONV = 4
GDN_CHUNK = 64

SGU_CHUNK = 128
SGU_GROUPS = 8
SGU_WIDTH = 2 * D_MODEL

VMEM_LIMIT_BYTES = 56 * 1024 * 1024
MOD_ROWS = 16
ROW_TILE = 512
NEG_BIG = -1e30


def _cparams(*sem):
    return pltpu.CompilerParams(dimension_semantics=sem, vmem_limit_bytes=VMEM_LIMIT_BYTES)


def _silu(x):
    return (0.5 * x) * (1.0 + jnp.tanh(0.5 * x))


def _identity(x):
    return x


def _gelu_erf(x):
    return 0.5 * x * (1.0 + lax.erf(x * (2.0 ** -0.5)))


def _dot(a, b):
    return jnp.dot(a, b, preferred_element_type=F32)


def _dot_nt(a, b):
    return lax.dot_general(a, b, (((1,), (1,)), ((), ())), preferred_element_type=F32)


def _dot_tn(a, b):
    return lax.dot_general(a, b, (((0,), (0,)), ((), ())), preferred_element_type=F32)


def _mods_kernel(cond_ref, w_ref, b_ref, o_ref):
    a = _silu(cond_ref[...]).astype(BF16)
    o_ref[...] = _dot(a, w_ref[...].astype(BF16)) + b_ref[...]


def ada_mods(cond, mod_w, mod_b):
    depth, d, n = mod_w.shape
    tn = 1024
    return pl.pallas_call(
        _mods_kernel,
        grid=(depth, n // tn),
        in_specs=[
            pl.BlockSpec((MOD_ROWS, d), lambda l, j: (0, 0)),
            pl.BlockSpec((None, d, tn), lambda l, j: (l, 0, j)),
            pl.BlockSpec((None, 1, tn), lambda l, j: (l, 0, j)),
        ],
        out_specs=pl.BlockSpec((None, MOD_ROWS, tn), lambda l, j: (l, 0, j)),
        out_shape=jax.ShapeDtypeStruct((depth, MOD_ROWS, n), F32),
        compiler_params=_cparams("arbitrary", "arbitrary"),
        name="ada_mods",
    )(cond, mod_w, mod_b.reshape(depth, 1, n))


def _mod_spec(row_fn, chunk, tn=None):
    if tn is None:
        return pl.BlockSpec((None, 1, D_MODEL), lambda i, j: (row_fn(i), 0, chunk))
    per = D_MODEL // tn
    return pl.BlockSpec((None, 1, tn), lambda i, j: (row_fn(i), 0, chunk * per + j))


def _row_fn(rows_per_batch, tm, is_ctx, batch):
    if is_ctx:
        return lambda i: batch
    per = rows_per_batch // tm
    return lambda i: i // per


NORM_ROWS = 16


def _norm_mod_into(x_ref, g_ref, sh_ref, sc_ref, h_ref):
    g = g_ref[...]
    shift = sh_ref[...]
    gain = 1.0 + sc_ref[...]

    def body(c, carry):
        rows = pl.ds(pl.multiple_of(c * NORM_ROWS, NORM_ROWS), NORM_ROWS)
        x = x_ref[rows, :]
        ms = jnp.mean(x * x, axis=-1, keepdims=True)
        h_ref[rows, :] = ((x * lax.rsqrt(ms + NORM_EPS) * g) * gain + shift).astype(BF16)
        return carry

    lax.fori_loop(0, x_ref.shape[0] // NORM_ROWS, body, 0, unroll=8)


def _linear_pre_kernel(x_ref, g_ref, sh_ref, sc_ref, w_ref, *rest, segs):
    out_refs = rest[:len(segs)]
    h_ref = rest[len(segs)]
    j = pl.program_id(1)

    @pl.when(j == 0)
    def _():
        _norm_mod_into(x_ref, g_ref, sh_ref, sc_ref, h_ref)

    start = 0
    for (ntiles, _, act), o_ref in zip(segs, out_refs):
        @pl.when((j >= start) & (j < start + ntiles))
        def _(o_ref=o_ref, act=act):
            o_ref[...] = act(_dot(h_ref[...], w_ref[...])).astype(o_ref.dtype)
        start += ntiles


def linear_pre(x, norm_g, mods, row_fn, w, segments, tn, tm=ROW_TILE, name="linear_pre"):
    m, d = x.shape
    n = w.shape[1]
    tm = min(tm, m)
    segs = []
    out_specs = []
    out_shapes = []
    start = 0
    for width, dtype, act in segments:
        nt = width // tn
        assert nt * tn == width
        segs.append((nt, dtype, act))
        out_specs.append(pl.BlockSpec(
            (tm, tn), lambda i, j, s=start, nt=nt: (i, jnp.clip(j - s, 0, nt - 1))))
        out_shapes.append(jax.ShapeDtypeStruct((m, width), dtype))
        start += nt
    assert start * tn == n
    outs = pl.pallas_call(
        functools.partial(_linear_pre_kernel, segs=tuple(segs)),
        grid=(m // tm, n // tn),
        in_specs=[
            pl.BlockSpec((tm, d), lambda i, j: (i, 0)),
            pl.BlockSpec((1, d), lambda i, j: (0, 0)),
            _mod_spec(row_fn, 0),
            _mod_spec(row_fn, 1),
            pl.BlockSpec((d, tn), lambda i, j: (0, j)),
        ],
        out_specs=out_specs,
        out_shape=out_shapes,
        scratch_shapes=[pltpu.VMEM((tm, d), BF16)],
        compiler_params=_cparams("arbitrary", "arbitrary"),
        name=name,
    )(x, norm_g.reshape(1, d), mods, mods, w)
    return outs


def _linear_post_kernel(a_ref, w_ref, res_ref, gate_ref, o_ref):
    o_ref[...] = res_ref[...] + gate_ref[...] * _dot(a_ref[...], w_ref[...])


def linear_post(a, w, res, mods, row_fn, gate_chunk, tn=1024, tm=ROW_TILE, name="linear_post"):
    m, k = a.shape
    n = w.shape[1]
    tm = min(tm, m)
    return pl.pallas_call(
        _linear_post_kernel,
        grid=(m // tm, n // tn),
        in_specs=[
            pl.BlockSpec((tm, k), lambda i, j: (i, 0)),
            pl.BlockSpec((k, tn), lambda i, j: (0, j)),
            pl.BlockSpec((tm, tn), lambda i, j: (i, j)),
            _mod_spec(row_fn, gate_chunk, tn),
        ],
        out_specs=pl.BlockSpec((tm, tn), lambda i, j: (i, j)),
        out_shape=jax.ShapeDtypeStruct((m, n), F32),
        compiler_params=_cparams("arbitrary", "arbitrary"),
        name=name,
    )(a, w, res, mods)


def _mlp_kernel(x_ref, g_ref, sh_ref, sc_ref, gate_ref, wu_ref, wd_ref, fg_ref, o_ref,
                h_ref, acc_ref, *, final_norm):
    j = pl.program_id(1)

    @pl.when(j == 0)
    def _():
        _norm_mod_into(x_ref, g_ref, sh_ref, sc_ref, h_ref)

    def hidden_step():
        a = jnp.maximum(_dot(h_ref[...], wu_ref[...]), 0.0)
        return _dot((a * a).astype(BF16), wd_ref[...])

    @pl.when(j == 0)
    def _():
        acc_ref[...] = hidden_step()

    @pl.when(j > 0)
    def _():
        acc_ref[...] += hidden_step()

    @pl.when(j == pl.num_programs(1) - 1)
    def _():
        y = x_ref[...] + gate_ref[...] * acc_ref[...]
        if final_norm:
            ms = jnp.mean(y * y, axis=-1, keepdims=True)
            y = y * lax.rsqrt(ms + NORM_EPS) * fg_ref[...]
        o_ref[...] = y


def mlp_block(x, norm_g, mods, row_fn, w_up, w_down, final_g=None, th=1024, tm=ROW_TILE, name="mlp"):
    m, d = x.shape
    hid = w_up.shape[1]
    tm = min(tm, m)
    final_norm = final_g is not None
    fg = (final_g if final_norm else norm_g).reshape(1, d)
    return pl.pallas_call(
        functools.partial(_mlp_kernel, final_norm=final_norm),
        grid=(m // tm, hid // th),
        in_specs=[
            pl.BlockSpec((tm, d), lambda i, j: (i, 0)),
            pl.BlockSpec((1, d), lambda i, j: (0, 0)),
            _mod_spec(row_fn, 3),
            _mod_spec(row_fn, 4),
            _mod_spec(row_fn, 5),
            pl.BlockSpec((d, th), lambda i, j: (0, j)),
            pl.BlockSpec((th, d), lambda i, j: (j, 0)),
            pl.BlockSpec((1, d), lambda i, j: (0, 0)),
        ],
        out_specs=pl.BlockSpec((tm, d), lambda i, j: (i, 0)),
        out_shape=jax.ShapeDtypeStruct((m, d), F32),
        scratch_shapes=[pltpu.VMEM((tm, d), BF16), pltpu.VMEM((tm, d), F32)],
        compiler_params=_cparams("arbitrary", "arbitrary"),
        name=name,
    )(x, norm_g.reshape(1, d), mods, mods, mods, w_up, w_down, fg)


def rope_tables(head_dim):
    rows = SEQ // GRID_W
    row = jnp.repeat(jnp.arange(rows, dtype=F32), GRID_W)
    col = jnp.tile(jnp.arange(GRID_W, dtype=F32), rows)
    axis_dim = head_dim // 2
    inv_freq = jnp.exp(-math.log(ROPE_BASE) * jnp.arange(0, axis_dim, 2, dtype=F32) / axis_dim)
    ang = jnp.concatenate([row[:, None] * inv_freq, col[:, None] * inv_freq], axis=-1)
    cos = jnp.cos(ang)
    sin = jnp.sin(ang)
    return jnp.concatenate([cos, cos], axis=-1), jnp.concatenate([-sin, sin], axis=-1)


def pair_split_columns(n_heads, head_dim, offset=0):
    idx = np.arange(head_dim)
    within = np.concatenate([idx[0::2], idx[1::2]])
    return (offset + np.arange(n_heads)[:, None] * head_dim + within[None, :]).reshape(-1)


def _rope(x, cos, sin_signed):
    partner = pltpu.roll(x, x.shape[-1] // 2, 1)
    return x * cos + partner * sin_signed


def _ret_tables():
    lg = np.log1p(-np.exp2(-5.0 - np.arange(RET_HEADS, dtype=np.float64)))[:, None]
    pos = np.arange(RET_CHUNK, dtype=np.float64)
    dist = np.abs(pos[:, None] - pos[None, :])
    intra = np.exp(lg[:, :, None] * dist) * (1.0 + np.eye(RET_CHUNK))
    q_f = np.exp(lg * (pos + 1.0))
    q_b = np.exp(lg * (RET_CHUNK - pos))
    k_f = np.exp(lg * (RET_CHUNK - 1.0 - pos))
    k_b = np.exp(lg * pos)
    dec = np.stack([q_f, q_b, k_f, k_b], axis=1)
    dec = np.broadcast_to(dec[..., None], dec.shape + (RET_DK,))
    cdec = np.broadcast_to(np.exp(lg * RET_CHUNK)[:, :, None], (RET_HEADS, 8, RET_DV))
    return (jnp.asarray(intra, F32), jnp.asarray(dec, F32), jnp.asarray(cdec, F32))


def _ret_kernel(ql_ref, kl_ref, vl_ref, gl_ref, qc_ref, kc_ref, vc_ref, gc_ref,
                cos_ref, sin_ref, intra_ref, dec_ref, cdec_ref, gn_ref,
                ol_ref, oc_ref, qs_ref, ks_ref, sb_ref, st_ref):
    C = RET_CHUNK
    n_lat = SEQ // C
    n_ctx = CTX_LEN // C
    kscale = RET_DK ** -0.5
    cd = cdec_ref[0:1, :]

    def rope_chunk(c, carry):
        rows = pl.ds(pl.multiple_of(c * C, C), C)
        cs = cos_ref[rows, :]
        sn = sin_ref[rows, :]
        qs_ref[rows, :] = _rope(ql_ref[rows, :], cs, sn)
        ks_ref[rows, :] = _rope(kl_ref[rows, :], cs, sn) * kscale
        return carry

    lax.fori_loop(0, n_lat, rope_chunk, 0)

    st_ref[...] = jnp.zeros_like(st_ref)

    def back_pair(k1, v1, slot1, k2, v2, slot2):
        st = st_ref[...]
        kv1 = _dot_tn((k1 * dec_ref[3]).astype(BF16), v1)
        kv2 = _dot_tn((k2 * dec_ref[3]).astype(BF16), v2)
        sb_ref[slot1] = st.astype(BF16)
        st = st * cd + kv1
        sb_ref[slot2] = st.astype(BF16)
        st_ref[...] = st * cd + kv2

    assert n_ctx == 2 and n_lat % 2 == 0
    back_pair(kc_ref[pl.ds(C, C), :] * kscale, vc_ref[pl.ds(C, C), :], 1,
              kc_ref[pl.ds(0, C), :] * kscale, vc_ref[pl.ds(0, C), :], 0)

    def back_lat(t, carry):
        c1 = n_lat - 1 - 2 * t
        r1 = pl.ds(pl.multiple_of(c1 * C, C), C)
        r2 = pl.ds(pl.multiple_of((c1 - 1) * C, C), C)
        back_pair(ks_ref[r1, :], vl_ref[r1, :], n_ctx + c1, ks_ref[r2, :], vl_ref[r2, :], n_ctx + c1 - 1)
        return carry

    lax.fori_loop(0, n_lat // 2, back_lat, 0)

    st_ref[...] = jnp.zeros_like(st_ref)

    def norm_gate(o, gate):
        ms = jnp.mean(o * o, axis=-1, keepdims=True)
        on = o * lax.rsqrt(ms + NORM_EPS) * gn_ref[...]
        return (on * gate.astype(F32)).astype(BF16)

    def fwd_pair(q1, k1, v1, gate1, slot1, q2, k2, v2, gate2, slot2):
        st = st_ref[...]
        s1 = _dot_nt(q1.astype(BF16), k1.astype(BF16))
        s2 = _dot_nt(q2.astype(BF16), k2.astype(BF16))
        kv1 = _dot_tn((k1 * dec_ref[2]).astype(BF16), v1)
        cross1 = (_dot((q1 * dec_ref[0]).astype(BF16), st.astype(BF16))
                  + _dot((q1 * dec_ref[1]).astype(BF16), sb_ref[slot1]))
        back2 = _dot((q2 * dec_ref[1]).astype(BF16), sb_ref[slot2])
        kv2 = _dot_tn((k2 * dec_ref[2]).astype(BF16), v2)
        o1 = _dot((s1 * intra_ref[...]).astype(BF16), v1) + cross1
        st = st * cd + kv1
        o2 = (_dot((s2 * intra_ref[...]).astype(BF16), v2) + back2
              + _dot((q2 * dec_ref[0]).astype(BF16), st.astype(BF16)))
        st_ref[...] = st * cd + kv2
        return norm_gate(o1, gate1), norm_gate(o2, gate2)

    r1 = pl.ds(0, C)
    r2 = pl.ds(C, C)
    oc_ref[r1, :], oc_ref[r2, :] = fwd_pair(
        qc_ref[r1, :], kc_ref[r1, :] * kscale, vc_ref[r1, :], gc_ref[r1, :], 0,
        qc_ref[r2, :], kc_ref[r2, :] * kscale, vc_ref[r2, :], gc_ref[r2, :], 1)

    def fwd_lat(t, carry):
        r1 = pl.ds(pl.multiple_of(2 * t * C, C), C)
        r2 = pl.ds(pl.multiple_of((2 * t + 1) * C, C), C)
        ol_ref[r1, :], ol_ref[r2, :] = fwd_pair(
            qs_ref[r1, :], ks_ref[r1, :], vl_ref[r1, :], gl_ref[r1, :], n_ctx + 2 * t,
            qs_ref[r2, :], ks_ref[r2, :], vl_ref[r2, :], gl_ref[r2, :], n_ctx + 2 * t + 1)
        return carry

    lax.fori_loop(0, n_lat // 2, fwd_lat, 0)


def retention_core(qk_lat, v_lat, g_lat, qk_ctx, v_ctx, g_ctx, gn_g, batch):
    H = RET_HEADS
    cos, sin = rope_tables(RET_DK)
    intra, dec, cdec = _ret_tables()
    n_chunks = (SEQ + CTX_LEN) // RET_CHUNK

    def specs(rows):
        return [
            pl.BlockSpec((rows, RET_DK), lambda b, h: (b, h)),
            pl.BlockSpec((rows, RET_DK), lambda b, h: (b, H + h)),
            pl.BlockSpec((rows, RET_DV), lambda b, h: (b, h)),
            pl.BlockSpec((rows, RET_DV), lambda b, h: (b, h)),
        ]

    return pl.pallas_call(
        _ret_kernel,
        grid=(batch, H),
        in_specs=specs(SEQ) + specs(CTX_LEN) + [
            pl.BlockSpec((SEQ, RET_DK), lambda b, h: (0, 0)),
            pl.BlockSpec((SEQ, RET_DK), lambda b, h: (0, 0)),
            pl.BlockSpec((None, RET_CHUNK, RET_CHUNK), lambda b, h: (h, 0, 0)),
            pl.BlockSpec((None, 4, RET_CHUNK, RET_DK), lambda b, h: (h, 0, 0, 0)),
            pl.BlockSpec((None, 8, RET_DV), lambda b, h: (h, 0, 0)),
            pl.BlockSpec((1, RET_DV), lambda b, h: (0, h)),
        ],
        out_specs=[
            pl.BlockSpec((SEQ, RET_DV), lambda b, h: (b, h)),
            pl.BlockSpec((CTX_LEN, RET_DV), lambda b, h: (b, h)),
        ],
        out_shape=[
            jax.ShapeDtypeStruct((batch * SEQ, H * RET_DV), BF16),
            jax.ShapeDtypeStruct((batch * CTX_LEN, H * RET_DV), BF16),
        ],
        scratch_shapes=[
            pltpu.VMEM((SEQ, RET_DK), F32),
            pltpu.VMEM((SEQ, RET_DK), F32),
            pltpu.VMEM((n_chunks, RET_DK, RET_DV), BF16),
            pltpu.VMEM((RET_DK, RET_DV), F32),
        ],
        compiler_params=_cparams("arbitrary", "arbitrary"),
        name="retention_core",
    )(qk_lat, qk_lat, v_lat, g_lat, qk_ctx, qk_ctx, v_ctx, g_ctx,
      cos, sin, intra, dec, cdec, gn_g.reshape(1, H * RET_DV))


def _att_kernel(sink_ref, ql_ref, kl_ref, vl_ref, qc_ref, kc_ref, vc_ref, cos_ref, sin_ref,
                ol_ref, oc_ref, kp_ref, vp_ref):
    BLK = ATT_BLOCK
    G = ATT_GROUP
    HD = ATT_HD
    n_blk = SEQ // BLK
    span = BLK + 2 * WINDOW
    scale = HD ** -0.5
    kvh = pl.program_id(1)

    zpad = jnp.zeros((WINDOW, HD), BF16)
    kp_ref[pl.ds(0, WINDOW), :] = zpad
    vp_ref[pl.ds(0, WINDOW), :] = zpad
    kp_ref[pl.ds(WINDOW + SEQ, WINDOW), :] = zpad
    vp_ref[pl.ds(WINDOW + SEQ, WINDOW), :] = zpad
    ctx0 = SEQ + 2 * WINDOW
    kp_ref[pl.ds(ctx0, CTX_LEN), :] = kc_ref[...].astype(BF16)
    vp_ref[pl.ds(ctx0, CTX_LEN), :] = vc_ref[...]

    def stage(c, carry):
        rows = pl.ds(pl.multiple_of(c * BLK, BLK), BLK)
        dst = pl.ds(pl.multiple_of(c * BLK + WINDOW, BLK), BLK)
        kp_ref[dst, :] = _rope(kl_ref[rows, :], cos_ref[rows, :], sin_ref[rows, :]).astype(BF16)
        vp_ref[dst, :] = vl_ref[rows, :]
        return carry

    lax.fori_loop(0, n_blk, stage, 0)

    kc_b = kp_ref[pl.ds(ctx0, CTX_LEN), :]
    vc_b = vp_ref[pl.ds(ctx0, CTX_LEN), :]

    def finish(parts, out_ref, rows, mask=None):
        pending = []
        for g in range(G):
            sink = jnp.full((BLK, 1), sink_ref[kvh * G + g], F32)
            logits = []
            for i, (s, _) in enumerate(parts):
                sg = s[g * BLK:(g + 1) * BLK, :]
                logits.append(jnp.where(mask, sg, NEG_BIG) if (mask is not None and i == 0) else sg)
            m = sink
            for sg in logits:
                m = jnp.maximum(m, jnp.max(sg, axis=-1, keepdims=True))
            den = jnp.exp(sink - m)
            acc = None
            for sg, (_, v) in zip(logits, parts):
                p = jnp.exp(sg - m)
                den = den + jnp.sum(p, axis=-1, keepdims=True)
                pv = _dot(p.astype(BF16), v)
                acc = pv if acc is None else acc + pv
            pending.append((acc, den))
        for g, (acc, den) in enumerate(pending):
            out_ref[rows, pl.ds(g * HD, HD)] = (acc / den).astype(BF16)

    qi = lax.broadcasted_iota(jnp.int32, (BLK, span), 0)
    col = lax.broadcasted_iota(jnp.int32, (BLK, span), 1)
    band = (col >= qi) & (col <= qi + 2 * WINDOW)

    def lat_block(bi, carry):
        rows = pl.ds(pl.multiple_of(bi * BLK, BLK), BLK)
        cs = cos_ref[rows, :]
        sn = sin_ref[rows, :]
        qs = [(_rope(ql_ref[rows, pl.ds(g * HD, HD)], cs, sn) * scale).astype(BF16) for g in range(G)]
        q4 = jnp.concatenate(qs, axis=0)
        krows = pl.ds(pl.multiple_of(bi * BLK, BLK), span)
        s_lat = _dot_nt(q4, kp_ref[krows, :])
        kpos = col + (bi * BLK - WINDOW)
        valid = band & (kpos >= 0) & (kpos < SEQ)
        s_ctx = _dot_nt(q4, kc_b)
        finish([(s_lat, vp_ref[krows, :]), (s_ctx, vc_b)], ol_ref, rows, mask=valid)
        return carry

    lax.fori_loop(0, n_blk, lat_block, 0)

    for bi in range(CTX_LEN // BLK):
        rows = pl.ds(bi * BLK, BLK)
        qs = [(qc_ref[rows, pl.ds(g * HD, HD)] * scale).astype(BF16) for g in range(G)]
        q4 = jnp.concatenate(qs, axis=0)
        finish([(_dot_nt(q4, kc_b), vc_b)], oc_ref, rows)


def attention_core(q_lat, k_lat, v_lat, q_ctx, k_ctx, v_ctx, sink, batch):
    cos, sin = rope_tables(ATT_HD)
    GW = ATT_GROUP * ATT_HD
    pad_rows = SEQ + 2 * WINDOW + CTX_LEN

    def specs(rows):
        return [
            pl.BlockSpec((rows, GW), lambda b, h: (b, h)),
            pl.BlockSpec((rows, ATT_HD), lambda b, h: (b, h)),
            pl.BlockSpec((rows, ATT_HD), lambda b, h: (b, h)),
        ]

    return pl.pallas_call(
        _att_kernel,
        grid=(batch, ATT_KV_HEADS),
        in_specs=[pl.BlockSpec(memory_space=pltpu.SMEM)] + specs(SEQ) + specs(CTX_LEN) + [
            pl.BlockSpec((SEQ, ATT_HD), lambda b, h: (0, 0)),
            pl.BlockSpec((SEQ, ATT_HD), lambda b, h: (0, 0)),
        ],
        out_specs=[
            pl.BlockSpec((SEQ, GW), lambda b, h: (b, h)),
            pl.BlockSpec((CTX_LEN, GW), lambda b, h: (b, h)),
        ],
        out_shape=[
            jax.ShapeDtypeStruct((batch * SEQ, ATT_HEADS * ATT_HD), BF16),
            jax.ShapeDtypeStruct((batch * CTX_LEN, ATT_HEADS * ATT_HD), BF16),
        ],
        scratch_shapes=[
            pltpu.VMEM((pad_rows, ATT_HD), BF16),
            pltpu.VMEM((pad_rows, ATT_HD), BF16),
        ],
        compiler_params=_cparams("arbitrary", "arbitrary"),
        name="attention_core",
    )(sink.astype(F32), q_lat, k_lat, v_lat, q_ctx, k_ctx, v_ctx, cos, sin)


def _sgu_kernel(u_ref, v_ref, g_ref, b_ref, ws_ref, bs_ref, o_ref, *, chunks):
    C = SGU_CHUNK
    gw = SGU_WIDTH // SGU_GROUPS
    for c in range(chunks):
        rows = pl.ds(c * C, C)
        v = v_ref[rows, :]
        mu = jnp.mean(v, axis=-1, keepdims=True)
        xc = v - mu
        var = jnp.mean(xc * xc, axis=-1, keepdims=True)
        vn = (xc * lax.rsqrt(var + NORM_EPS) * g_ref[...] + b_ref[...]).astype(BF16)
        for g in range(SGU_GROUPS):
            cols = pl.ds(g * gw, gw)
            t = _dot(ws_ref[g], vn[:, g * gw:(g + 1) * gw]) + bs_ref[g]
            o_ref[rows, cols] = (u_ref[rows, cols].astype(F32) * t).astype(BF16)


def sgu_core(u, v, ln_g, ln_b, w_s, b_s, chunks=2):
    m = u.shape[0]
    tm = chunks * SGU_CHUNK
    return pl.pallas_call(
        functools.partial(_sgu_kernel, chunks=chunks),
        grid=(m // tm,),
        in_specs=[
            pl.BlockSpec((tm, SGU_WIDTH), lambda i: (i, 0)),
            pl.BlockSpec((tm, SGU_WIDTH), lambda i: (i, 0)),
            pl.BlockSpec((1, SGU_WIDTH), lambda i: (0, 0)),
            pl.BlockSpec((1, SGU_WIDTH), lambda i: (0, 0)),
            pl.BlockSpec((SGU_GROUPS, SGU_CHUNK, SGU_CHUNK), lambda i: (0, 0, 0)),
            pl.BlockSpec((SGU_GROUPS, SGU_CHUNK, 1), lambda i: (0, 0, 0)),
        ],
        out_specs=pl.BlockSpec((tm, SGU_WIDTH), lambda i: (i, 0)),
        out_shape=jax.ShapeDtypeStruct((m, SGU_WIDTH), BF16),
        compiler_params=_cparams("arbitrary"),
        name="sgu_core",
    )(u, v, ln_g.reshape(1, -1), ln_b.reshape(1, -1), w_s.astype(BF16),
      b_s.reshape(SGU_GROUPS, SGU_CHUNK, 1))


GDN_ROWS = CTX_LEN + SEQ
GDN_NCHUNK = GDN_ROWS // GDN_CHUNK
GDN_CONV_ROWS = 256
GDN_HALO = 8
GDN_PREP_TILES = 3


def _gdn_kernel(ql_ref, kl_ref, vl_ref, zl_ref, qc_ref, kc_ref, vc_ref, ba_ref,
                cwq_ref, cwk_ref, cwv_ref, alog_ref, dt_ref, ng_ref,
                ol_ref,
                xp_ref, qn_ref, kn_ref, vv_ref, gt_ref, qp_ref, g_ref, b_ref,
                el_ref, of_ref, ob_ref, s_ref):
    C = GDN_CHUNK
    DK = GDN_DK
    R = GDN_CONV_ROWS
    HALO = GDN_HALO

    def conv_silu(src_ref, n_rows, cw_ref, width, store):
        cols = pl.ds(0, width)
        xp_ref[pl.ds(0, HALO), cols] = jnp.zeros((HALO, width), F32)
        xp_ref[pl.ds(HALO, n_rows), cols] = src_ref[...]
        xp_ref[pl.ds(HALO + n_rows, HALO), cols] = jnp.zeros((HALO, width), F32)
        w = cw_ref[...]
        n = R + 2 * HALO

        def body(c, carry):
            r0 = pl.multiple_of(c * R, R)
            x = xp_ref[pl.ds(r0, n), cols]
            y = (w[2:3] * x + w[1:2] * pltpu.roll(x, 1, 0) + w[0:1] * pltpu.roll(x, 2, 0)
                 + w[3:4] * pltpu.roll(x, n - 1, 0))[HALO:HALO + R]
            store(r0, _silu(y))
            return carry

        lax.fori_loop(0, n_rows // R, body, 0)

    def unit(y):
        return y * lax.rsqrt(jnp.sum(y * y, axis=-1, keepdims=True) + 1e-6)

    def store_q(off):
        def f(r0, y):
            qn_ref[pl.ds(pl.multiple_of(off + r0, R), R), :] = unit(y) * (DK ** -0.5)
        return f

    def store_k(off):
        def f(r0, y):
            kn_ref[pl.ds(pl.multiple_of(off + r0, R), R), :] = unit(y)
        return f

    def store_v(off):
        def f(r0, y):
            vv_ref[pl.ds(pl.multiple_of(off + r0, R), R), :] = y
        return f

    conv_silu(qc_ref, CTX_LEN, cwq_ref, DK, store_q(0))
    conv_silu(kc_ref, CTX_LEN, cwk_ref, DK, store_k(0))
    conv_silu(vc_ref, CTX_LEN, cwv_ref, 2 * GDN_DV, store_v(0))
    conv_silu(ql_ref, SEQ, cwq_ref, DK, store_q(CTX_LEN))
    conv_silu(kl_ref, SEQ, cwk_ref, DK, store_k(CTX_LEN))
    conv_silu(vl_ref, SEQ, cwv_ref, 2 * GDN_DV, store_v(CTX_LEN))

    raw = ba_ref[...]
    xg = raw + dt_ref[...]
    softplus = jnp.maximum(xg, 0.0) + jnp.log1p(jnp.exp(-jnp.abs(xg)))
    g = -jnp.exp(alog_ref[...]) * softplus
    pos = lax.broadcasted_iota(jnp.int32, raw.shape, 1) % C
    pre = g
    suf = g
    shift = 1
    while shift < C:
        pre = pre + jnp.where(pos >= shift, pltpu.roll(pre, shift, 1), 0.0)
        suf = suf + jnp.where(pos + shift < C, pltpu.roll(suf, GDN_ROWS - shift, 1), 0.0)
        shift *= 2
    row = lax.broadcasted_iota(jnp.int32, raw.shape, 0)
    gt_ref[...] = jnp.where(row < 4, jax.nn.sigmoid(raw), jnp.where(row < 6, pre, suf))

    PK = 2 * C
    ri = lax.broadcasted_iota(jnp.int32, (C, PK), 0)
    li = lax.broadcasted_iota(jnp.int32, (C, PK), 1)
    ci = li % C
    left = li < C
    eye = (ri == ci).astype(F32)
    incl = (ri >= ci, ri <= ci)
    strict = (ri > ci, ri < ci)
    half_sel = (li == ri, li == ri + C)

    def to_col(rowvec, half):
        return jnp.sum(jnp.where(half_sel[half], rowvec, 0.0), axis=1, keepdims=True)

    def block_diag(m, zero):
        return jnp.concatenate([jnp.concatenate([m[0], zero], axis=1),
                                jnp.concatenate([zero, m[1]], axis=1)], axis=0)

    def dot3_packed(x, m):
        bd = jnp.concatenate([jnp.where(left, m, 0.0), jnp.where(left, 0.0, m)], axis=0)
        x_hi = x.astype(BF16)
        bd_hi = bd.astype(BF16)
        x_lo = (x - x_hi.astype(F32)).astype(BF16)
        bd_lo = (bd - bd_hi.astype(F32)).astype(BF16)
        rhs = jnp.concatenate([jnp.concatenate([bd_hi, bd_lo], axis=1),
                               jnp.concatenate([bd_hi, jnp.zeros_like(bd_hi)], axis=1)], axis=0)
        out = _dot(jnp.concatenate([x_hi, x_lo], axis=1), rhs)
        return out[:, :PK] + out[:, PK:]

    def prep_pair(p, carry):
        zero_uw = jnp.zeros((C, 2 * GDN_DV), BF16)
        chains = []
        for sub in range(2 * GDN_PREP_TILES):
            tile, half = divmod(sub, 2)
            if half == 0:
                lanes = pl.ds(pl.multiple_of((p * GDN_PREP_TILES + tile) * PK, PK), PK)
                gates = gt_ref[:, lanes]
                gates_sw = pltpu.roll(gates, C, 1)
            chunk = (p * GDN_PREP_TILES + tile) * 2 + half
            rows = pl.ds(pl.multiple_of(chunk * C, C), C)
            q = qn_ref[rows, :]
            k = kn_ref[rows, :]
            v2 = vv_ref[rows, :]
            qk_kk = _dot_nt(jnp.concatenate([q, k], axis=0).astype(BF16),
                            jnp.concatenate([k, k], axis=0).astype(BF16))
            for d in range(2):
                def packed_row(r, half=half, d=d):
                    a = (gates if half == 0 else gates_sw)[r + 2 * d:r + 2 * d + 1]
                    b = (gates_sw if half == 0 else gates)[r + 2 * d + 1:r + 2 * d + 2]
                    return jnp.where(left[0:1], a, b)
                edge = half * C + (C - 1 if d == 0 else 0)
                chains.append(dict(
                    d=d, n=chunk, rows=rows, q=q, k=k, v2=v2, qk=qk_kk[:C], kk=qk_kk[C:],
                    grow=packed_row(4),
                    bcols=[to_col(gates[2 * d + j:2 * d + j + 1], half) for j in range(2)],
                    gcols=[to_col(gates[4 + 2 * d + j:5 + 2 * d + j], half) for j in range(2)],
                    glast=[gates[4 + 2 * d + j:5 + 2 * d + j, edge:edge + 1] for j in range(2)]))
        for ch in chains:
            d = ch["d"]
            bcol = jnp.where(left, ch["bcols"][0], ch["bcols"][1])
            gcol = jnp.where(left, ch["gcols"][0], ch["gcols"][1])
            ch["dec"] = jnp.exp(jnp.where(incl[d], gcol - ch["grow"], NEG_BIG))
            nmat = jnp.where(strict[d], -(ch["kk"] * bcol * ch["dec"]), 0.0)
            ch["pm"] = eye + nmat
            ch["nmat"] = nmat
        for ch in chains:
            ch["mm"] = dot3_packed(ch["nmat"], ch["nmat"])
        for _ in range(4):
            for ch in chains:
                r = dot3_packed(jnp.concatenate([ch["pm"], ch["mm"]], axis=0), ch["mm"])
                ch["pm"] = ch["pm"] + r[:C]
                ch["mm"] = r[C:]
        for ch in chains:
            ch["r"] = dot3_packed(ch["pm"], ch["mm"])
        for ch in chains:
            tmat = (ch["pm"] + ch["r"]).astype(BF16)
            ch["eg"] = [jnp.exp(g) for g in ch["gcols"]]
            rhs = [jnp.concatenate([ch["v2"][:, j * GDN_DV:(j + 1) * GDN_DV] * ch["bcols"][j],
                                    ch["k"] * (ch["bcols"][j] * ch["eg"][j])], axis=1).astype(BF16)
                   for j in range(2)]
            ch["uw"] = _dot(tmat, block_diag(rhs, zero_uw)).astype(BF16)
        for ch in chains:
            uw = [ch["uw"][:, j * 2 * GDN_DV:(j + 1) * 2 * GDN_DV] for j in range(2)]
            amat = (ch["qk"] * ch["dec"]).astype(BF16)
            ch["a_uw"] = _dot(amat, block_diag(uw, zero_uw))
            ch["k_uw"] = [_dot_tn((ch["k"] * jnp.exp(ch["glast"][j] - ch["gcols"][j])).astype(BF16), uw[j])
                          for j in range(2)]
        for ch in chains:
            n, rows = ch["n"], ch["rows"]
            out_ref = of_ref if ch["d"] == 0 else ob_ref
            for j in range(2):
                combo = 2 * ch["d"] + j
                a_u = ch["a_uw"][:, (2 * j) * GDN_DV:(2 * j + 1) * GDN_DV]
                a_w = ch["a_uw"][:, (2 * j + 1) * GDN_DV:(2 * j + 2) * GDN_DV]
                qp_ref[combo, rows, :] = (ch["q"] * ch["eg"][j] - a_w).astype(BF16)
                out_ref[rows, pl.ds(j * GDN_DV, GDN_DV)] = a_u
                b_ref[combo, n] = ch["k_uw"][j][:, :GDN_DV]
                g_ref[combo, n] = ch["k_uw"][j][:, GDN_DV:].astype(BF16)
                el_ref[combo, n] = jnp.broadcast_to(jnp.exp(ch["glast"][j]), (8, GDN_DV))
        return carry

    lax.fori_loop(0, GDN_NCHUNK // (2 * GDN_PREP_TILES), prep_pair, 0)

    s_ref[...] = jnp.zeros_like(s_ref)

    def rec_pair(nf, nb):
        work = []
        for combo in range(4):
            n = nf if combo < 2 else nb
            rows = pl.ds(pl.multiple_of(n * C, C), C)
            out_ref = of_ref if combo < 2 else ob_ref
            cols = pl.ds((combo % 2) * GDN_DV, GDN_DV)
            lhs = jnp.concatenate([g_ref[combo, n], qp_ref[combo, rows, :]], axis=0)
            work.append((combo, out_ref, rows, cols, lhs, s_ref[combo], b_ref[combo, n],
                         el_ref[combo, n][0:1, :], out_ref[rows, cols]))
        done = []
        for combo, out_ref, rows, cols, lhs, st, bmat, e_last, o0 in work:
            r = _dot(lhs, st.astype(BF16))
            done.append((combo, out_ref, rows, cols, st * e_last + bmat - r[:GDN_DK], o0 + r[GDN_DK:]))
        for combo, out_ref, rows, cols, st_new, o in done:
            s_ref[combo] = st_new
            out_ref[rows, cols] = o

    n_ctx = CTX_LEN // C
    for t in range(n_ctx):
        rec_pair(t, n_ctx - 1 - t)

    def rec_lat(t, carry):
        rec_pair(n_ctx + t, GDN_NCHUNK - 1 - t)
        return carry

    lax.fori_loop(0, SEQ // C, rec_lat, 0)

    def fin(c, carry):
        src = pl.ds(pl.multiple_of(CTX_LEN + c * R, R), R)
        dst = pl.ds(pl.multiple_of(c * R, R), R)
        for j in range(2):
            cols = pl.ds(j * GDN_DV, GDN_DV)
            o = of_ref[src, cols] + ob_ref[src, cols]
            ms = jnp.mean(o * o, axis=-1, keepdims=True)
            on = o * lax.rsqrt(ms + NORM_EPS) * ng_ref[...]
            ol_ref[dst, cols] = (on * zl_ref[dst, cols].astype(F32)).astype(BF16)
        return carry

    lax.fori_loop(0, SEQ // R, fin, 0)


def gdn_core(mixed_lat, z_lat, mixed_ctx, ba, conv_w, a_log, dt_bias, norm_g, batch):
    HK = GDN_K_HEADS
    DK = GDN_DK
    VW = 2 * GDN_DV
    v_blk0 = (2 * HK * DK) // VW

    def tab(t):
        t = t.reshape(2, HK, 2).transpose(1, 0, 2).reshape(HK, 4)
        return jnp.concatenate([jnp.zeros_like(t), t], axis=1).reshape(HK, 8, 1).astype(F32)

    def specs(rows):
        return [
            pl.BlockSpec((rows, DK), lambda b, h: (b, h)),
            pl.BlockSpec((rows, DK), lambda b, h: (b, HK + h)),
            pl.BlockSpec((rows, VW), lambda b, h: (b, v_blk0 + h)),
        ]

    nc = GDN_NCHUNK
    return pl.pallas_call(
        _gdn_kernel,
        grid=(batch, HK),
        in_specs=specs(SEQ) + [pl.BlockSpec((SEQ, VW), lambda b, h: (b, h))] + specs(CTX_LEN) + [
            pl.BlockSpec((None, None, 8, GDN_ROWS), lambda b, h: (b, h, 0, 0)),
            pl.BlockSpec((GDN_CONV, DK), lambda b, h: (0, h)),
            pl.BlockSpec((GDN_CONV, DK), lambda b, h: (0, HK + h)),
            pl.BlockSpec((GDN_CONV, VW), lambda b, h: (0, v_blk0 + h)),
            pl.BlockSpec((None, 8, 1), lambda b, h: (h, 0, 0)),
            pl.BlockSpec((None, 8, 1), lambda b, h: (h, 0, 0)),
            pl.BlockSpec((1, GDN_DV), lambda b, h: (0, 0)),
        ],
        out_specs=pl.BlockSpec((SEQ, VW), lambda b, h: (b, h)),
        out_shape=jax.ShapeDtypeStruct((batch * SEQ, GDN_V_HEADS * GDN_DV), BF16),
        scratch_shapes=[
            pltpu.VMEM((SEQ + 2 * GDN_HALO, VW), F32),
            pltpu.VMEM((GDN_ROWS, DK), F32),
            pltpu.VMEM((GDN_ROWS, DK), F32),
            pltpu.VMEM((GDN_ROWS, VW), F32),
            pltpu.VMEM((8, GDN_ROWS), F32),
            pltpu.VMEM((4, GDN_ROWS, DK), BF16),
            pltpu.VMEM((4, nc, DK, GDN_DV), BF16),
            pltpu.VMEM((4, nc, DK, GDN_DV), F32),
            pltpu.VMEM((4, nc, 8, GDN_DV), F32),
            pltpu.VMEM((GDN_ROWS, VW), F32),
            pltpu.VMEM((GDN_ROWS, VW), F32),
            pltpu.VMEM((4, DK, GDN_DV), F32),
        ],
        compiler_params=_cparams("arbitrary", "arbitrary"),
        name="gdn_core",
    )(mixed_lat, mixed_lat, mixed_lat, z_lat, mixed_ctx, mixed_ctx, mixed_ctx, ba,
      conv_w, conv_w, conv_w, tab(a_log), tab(dt_bias), norm_g.reshape(1, GDN_DV))


def _finish_layer(o_lat, o_ctx, xl, xc, w_out, norm2_g, up, down, mods, rows, final_g=None):
    lat_row, ctx_row = rows
    w_out = w_out.astype(BF16)
    up = up.astype(BF16)
    down = down.astype(BF16)
    xl = linear_post(o_lat, w_out, xl, mods, lat_row, 2, name="out_proj_lat")
    xl = mlp_block(xl, norm2_g, mods, lat_row, up, down, final_g=final_g, name="mlp_lat")
    if o_ctx is not None:
        xc = linear_post(o_ctx, w_out, xc, mods, ctx_row, 2, name="out_proj_ctx")
        xc = mlp_block(xc, norm2_g, mods, ctx_row, up, down, name="mlp_ctx")
    return xl, xc


def retention_layer(xl, xc, mods, rows, batch, norm1_g, norm2_g, w_in, gn_g, w_out, up, down):
    lat_row, ctx_row = rows
    qkw = 2 * RET_HEADS * RET_DK
    vw = RET_HEADS * RET_DV
    cols = np.concatenate([pair_split_columns(2 * RET_HEADS, RET_DK), np.arange(qkw, qkw + 2 * vw)])
    w_in = w_in[:, cols].astype(BF16)
    segs = [(qkw, F32, _identity), (vw, BF16, _identity), (vw, BF16, _silu)]
    qk_l, v_l, g_l = linear_pre(xl, norm1_g, mods, lat_row, w_in, segs, 1024, name="ret_in_lat")
    qk_c, v_c, g_c = linear_pre(xc, norm1_g, mods, ctx_row, w_in, segs, 1024, name="ret_in_ctx")
    o_l, o_c = retention_core(qk_l, v_l, g_l, qk_c, v_c, g_c, gn_g, batch)
    return _finish_layer(o_l, o_c, xl, xc, w_out, norm2_g, up, down, mods, rows)


def attention_layer(xl, xc, mods, rows, batch, norm1_g, norm2_g, w_in, sink, w_out, up, down):
    lat_row, ctx_row = rows
    qw = ATT_HEADS * ATT_HD
    kvw = ATT_KV_HEADS * ATT_HD
    cols = np.concatenate([pair_split_columns(ATT_HEADS + ATT_KV_HEADS, ATT_HD),
                           np.arange(qw + kvw, qw + 2 * kvw)])
    w_in = w_in[:, cols].astype(BF16)
    segs = [(qw, F32, _identity), (kvw, F32, _identity), (kvw, BF16, _identity)]
    q_l, k_l, v_l = linear_pre(xl, norm1_g, mods, lat_row, w_in, segs, 512, name="att_in_lat")
    q_c, k_c, v_c = linear_pre(xc, norm1_g, mods, ctx_row, w_in, segs, 512, name="att_in_ctx")
    o_l, o_c = attention_core(q_l, k_l, v_l, q_c, k_c, v_c, sink, batch)
    return _finish_layer(o_l, o_c, xl, xc, w_out, norm2_g, up, down, mods, rows)


def gdn_layer(xl, xc, mods, rows, batch, norm1_g, norm2_g, w_in, conv_w, a_log, dt_bias, norm_g,
              w_out, up, down):
    lat_row, ctx_row = rows
    conv_ch = 2 * GDN_K_HEADS * GDN_DK + GDN_V_HEADS * GDN_DV
    zw = GDN_V_HEADS * GDN_DV
    w_main = w_in[:, :conv_ch + zw].astype(BF16)
    w_gate = w_in[:, conv_ch + zw:].astype(BF16)
    segs = [(conv_ch, F32, _identity), (zw, BF16, _silu)]
    gsegs = [(4 * GDN_V_HEADS, F32, _identity)]
    mx_l, z_l = linear_pre(xl, norm1_g, mods, lat_row, w_main, segs, 1024, name="gdn_in_lat")
    mx_c, _ = linear_pre(xc, norm1_g, mods, ctx_row, w_main, segs, 1024, name="gdn_in_ctx")
    (ba_l,) = linear_pre(xl, norm1_g, mods, lat_row, w_gate, gsegs, 128, name="gdn_gate_lat")
    (ba_c,) = linear_pre(xc, norm1_g, mods, ctx_row, w_gate, gsegs, 128, name="gdn_gate_ctx")
    ba = jnp.concatenate([ba_c.reshape(batch, CTX_LEN, -1), ba_l.reshape(batch, SEQ, -1)], axis=1)
    ba = ba.reshape(batch, GDN_ROWS, 2, 2, GDN_K_HEADS, 2).transpose(0, 4, 2, 3, 5, 1)
    ba = ba.reshape(batch, GDN_K_HEADS, 8, GDN_ROWS)
    o_l = gdn_core(mx_l, z_l, mx_c, ba, conv_w, a_log, dt_bias, norm_g, batch)
    return _finish_layer(o_l, None, xl, xc, w_out, norm2_g, up, down, mods, rows)


def sgu_layer(xl, xc, mods, rows, norm1_g, norm2_g, w_in, ln_g, ln_b, w_s, b_s, w_out, up, down,
              final_g):
    lat_row, _ = rows
    segs = [(SGU_WIDTH, BF16, _gelu_erf), (SGU_WIDTH, F32, _gelu_erf)]
    u_l, v_l = linear_pre(xl, norm1_g, mods, lat_row, w_in.astype(BF16), segs, 1024, name="sgu_in_lat")
    o_l = sgu_core(u_l, v_l, ln_g, ln_b, w_s, b_s)
    return _finish_layer(o_l, None, xl, xc, w_out, norm2_g, up, down, mods, rows, final_g=final_g)


def kernel(x, c, ctx, c_ctx, mod_w, mod_b, norm1_g, norm2_g, mlp_up, mlp_down, final_g,
           ret_w_in, ret_gn_g, ret_w_out, att_w_in, att_sink, att_w_out,
           gdn_w_in, gdn_conv_w, gdn_a_log, gdn_dt_bias, gdn_norm_g, gdn_w_out,
           sgu_w_in, sgu_ln_g, sgu_ln_b, sgu_w_s, sgu_b_s, sgu_w_out):
    batch, seq, d = x.shape
    assert (seq, d, ctx.shape[1]) == (SEQ, D_MODEL, CTX_LEN) and batch < MOD_ROWS
    xl = x.reshape(batch * SEQ, d)
    xc = ctx.reshape(batch * CTX_LEN, d)

    cond = jnp.zeros((MOD_ROWS, d), F32).at[:batch].set(c).at[batch].set(c_ctx)
    mods_all = ada_mods(cond, mod_w, mod_b).reshape(DEPTH, MOD_ROWS, 1, 6 * d)
    rows = (_row_fn(SEQ, ROW_TILE, False, batch), _row_fn(CTX_LEN, ROW_TILE, True, batch))

    xl, xc = retention_layer(xl, xc, mods_all[0], rows, batch, norm1_g[0], norm2_g[0],
                             ret_w_in[0], ret_gn_g[0], ret_w_out[0], mlp_up[0], mlp_down[0])
    xl, xc = attention_layer(xl, xc, mods_all[1], rows, batch, norm1_g[1], norm2_g[1],
                             att_w_in[0], att_sink[0], att_w_out[0], mlp_up[1], mlp_down[1])
    xl, xc = gdn_layer(xl, xc, mods_all[2], rows, batch, norm1_g[2], norm2_g[2],
                       gdn_w_in[0], gdn_conv_w[0], gdn_a_log[0], gdn_dt_bias[0], gdn_norm_g[0],
                       gdn_w_out[0], mlp_up[2], mlp_down[2])
    xl, xc = sgu_layer(xl, xc, mods_all[3], rows, norm1_g[3], norm2_g[3],
                       sgu_w_in[0], sgu_ln_g[0], sgu_ln_b[0], sgu_w_s[0], sgu_b_s[0], sgu_w_out[0],
                       mlp_up[3], mlp_down[3], final_g)
    return xl.reshape(batch, SEQ, d)
```

```python
import functools
import math

import numpy as np
import jax
import jax.numpy as jnp
from jax import lax
from jax.experimental import pallas as pl
from jax.experimental.pallas import tpu as pltpu

F32 = jnp.float32
BF16 = jnp.bfloat16

D_MODEL = 2048
SEQ = 2048
CTX_LEN = 256
GRID_W = 64
DEPTH = 4
NORM_EPS = 1e-6
ROPE_BASE = 10000.0
MLP_HIDDEN = 4 * D_MODEL

RET_HEADS = 8
RET_DK = 256
RET_DV = 512
RET_CHUNK = 128

ATT_HEADS = 16
ATT_KV_HEADS = 4
ATT_HD = 128
ATT_GROUP = 4
WINDOW = 128
ATT_BLOCK = 128

GDN_K_HEADS = 16
GDN_V_HEADS = 32
GDN_DK = 128
GDN_DV = 128
GDN_CONV = 4
GDN_CHUNK = 64

SGU_CHUNK = 128
SGU_GROUPS = 8
SGU_WIDTH = 2 * D_MODEL

VMEM_LIMIT_BYTES = 56 * 1024 * 1024
MOD_ROWS = 16
ROW_TILE = 512
PRE_ROW_TILE = 1024
PRE_COL_TILE = 512
NEG_BIG = -1e30


def _cparams(*sem):
    return pltpu.CompilerParams(dimension_semantics=sem, vmem_limit_bytes=VMEM_LIMIT_BYTES)


def _silu(x):
    return (0.5 * x) * (1.0 + jnp.tanh(0.5 * x))


def _identity(x):
    return x


def _gelu_erf(x):
    return 0.5 * x * (1.0 + lax.erf(x * (2.0 ** -0.5)))


def _dot(a, b):
    return jnp.dot(a, b, preferred_element_type=F32)


def _dot_nt(a, b):
    return lax.dot_general(a, b, (((1,), (1,)), ((), ())), preferred_element_type=F32)


def _dot_tn(a, b):
    return lax.dot_general(a, b, (((0,), (0,)), ((), ())), preferred_element_type=F32)


def _mods_kernel(cond_ref, w_ref, b_ref, o_ref):
    a = _silu(cond_ref[...]).astype(BF16)
    o_ref[...] = _dot(a, w_ref[...].astype(BF16)) + b_ref[...]


def ada_mods(cond, mod_w, mod_b):
    depth, d, n = mod_w.shape
    tn = 1024
    return pl.pallas_call(
        _mods_kernel,
        grid=(depth, n // tn),
        in_specs=[
            pl.BlockSpec((MOD_ROWS, d), lambda l, j: (0, 0)),
            pl.BlockSpec((None, d, tn), lambda l, j: (l, 0, j)),
            pl.BlockSpec((None, 1, tn), lambda l, j: (l, 0, j)),
        ],
        out_specs=pl.BlockSpec((None, MOD_ROWS, tn), lambda l, j: (l, 0, j)),
        out_shape=jax.ShapeDtypeStruct((depth, MOD_ROWS, n), F32),
        compiler_params=_cparams("arbitrary", "arbitrary"),
        name="ada_mods",
    )(cond, mod_w, mod_b.reshape(depth, 1, n))


def _mod_spec(row_fn, chunk, tn=None):
    if tn is None:
        return pl.BlockSpec((None, 1, D_MODEL), lambda i, j: (row_fn(i), 0, chunk))
    per = D_MODEL // tn
    return pl.BlockSpec((None, 1, tn), lambda i, j: (row_fn(i), 0, chunk * per + j))


def _row_fn(rows_per_batch, is_ctx, batch, tm):
    if is_ctx:
        return lambda i: batch
    per = rows_per_batch // tm
    return lambda i: i // per


NORM_ROWS = 16


def _norm_mod_into(x_ref, g_ref, sh_ref, sc_ref, h_ref):
    g = g_ref[...]
    shift = sh_ref[...]
    gain = 1.0 + sc_ref[...]

    def body(c, carry):
        rows = pl.ds(pl.multiple_of(c * NORM_ROWS, NORM_ROWS), NORM_ROWS)
        x = x_ref[rows, :]
        ms = jnp.mean(x * x, axis=-1, keepdims=True)
        h_ref[rows, :] = ((x * lax.rsqrt(ms + NORM_EPS) * g) * gain + shift).astype(BF16)
        return carry

    lax.fori_loop(0, x_ref.shape[0] // NORM_ROWS, body, 0, unroll=8)


def _linear_pre_kernel(x_ref, g_ref, sh_ref, sc_ref, w_ref, *rest, segs):
    out_refs = rest[:len(segs)]
    h_ref = rest[len(segs)]
    j = pl.program_id(1)

    @pl.when(j == 0)
    def _():
        _norm_mod_into(x_ref, g_ref, sh_ref, sc_ref, h_ref)

    start = 0
    for (ntiles, _, act), o_ref in zip(segs, out_refs):
        @pl.when((j >= start) & (j < start + ntiles))
        def _(o_ref=o_ref, act=act):
            o_ref[...] = act(_dot(h_ref[...], w_ref[...])).astype(o_ref.dtype)
        start += ntiles


def linear_pre(x, norm_g, mods, row_fn, w, segments, tn=PRE_COL_TILE, tm=PRE_ROW_TILE, name="linear_pre"):
    m, d = x.shape
    n = w.shape[1]
    tm = min(tm, m)
    row_fn = row_fn(tm)
    segs = []
    out_specs = []
    out_shapes = []
    start = 0
    for width, dtype, act in segments:
        nt = width // tn
        assert nt * tn == width
        segs.append((nt, dtype, act))
        out_specs.append(pl.BlockSpec(
            (tm, tn), lambda i, j, s=start, nt=nt: (i, jnp.clip(j - s, 0, nt - 1))))
        out_shapes.append(jax.ShapeDtypeStruct((m, width), dtype))
        start += nt
    assert start * tn == n
    outs = pl.pallas_call(
        functools.partial(_linear_pre_kernel, segs=tuple(segs)),
        grid=(m // tm, n // tn),
        in_specs=[
            pl.BlockSpec((tm, d), lambda i, j: (i, 0)),
            pl.BlockSpec((1, d), lambda i, j: (0, 0)),
            _mod_spec(row_fn, 0),
            _mod_spec(row_fn, 1),
            pl.BlockSpec((d, tn), lambda i, j: (0, j)),
        ],
        out_specs=out_specs,
        out_shape=out_shapes,
        scratch_shapes=[pltpu.VMEM((tm, d), BF16)],
        compiler_params=_cparams("arbitrary", "arbitrary"),
        name=name,
    )(x, norm_g.reshape(1, d), mods, mods, w)
    return outs


def _linear_post_kernel(a_ref, w_ref, res_ref, gate_ref, o_ref):
    o_ref[...] = res_ref[...] + gate_ref[...] * _dot(a_ref[...], w_ref[...])


def linear_post(a, w, res, mods, row_fn, gate_chunk, tn=PRE_COL_TILE, tm=PRE_ROW_TILE, name="linear_post"):
    m, k = a.shape
    n = w.shape[1]
    tm = min(tm, m)
    row_fn = row_fn(tm)
    return pl.pallas_call(
        _linear_post_kernel,
        grid=(m // tm, n // tn),
        in_specs=[
            pl.BlockSpec((tm, k), lambda i, j: (i, 0)),
            pl.BlockSpec((k, tn), lambda i, j: (0, j)),
            pl.BlockSpec((tm, tn), lambda i, j: (i, j)),
            _mod_spec(row_fn, gate_chunk, tn),
        ],
        out_specs=pl.BlockSpec((tm, tn), lambda i, j: (i, j)),
        out_shape=jax.ShapeDtypeStruct((m, n), F32),
        compiler_params=_cparams("arbitrary", "arbitrary"),
        name=name,
    )(a, w, res, mods)


def _mlp_kernel(x_ref, g_ref, sh_ref, sc_ref, gate_ref, wu_ref, wd_ref, fg_ref, o_ref,
                h_ref, acc_ref, *, final_norm):
    j = pl.program_id(1)

    @pl.when(j == 0)
    def _():
        _norm_mod_into(x_ref, g_ref, sh_ref, sc_ref, h_ref)

    def hidden_step():
        a = jnp.maximum(_dot(h_ref[...], wu_ref[...]), 0.0)
        return _dot((a * a).astype(BF16), wd_ref[...])

    @pl.when(j == 0)
    def _():
        acc_ref[...] = hidden_step()

    @pl.when(j > 0)
    def _():
        acc_ref[...] += hidden_step()

    @pl.when(j == pl.num_programs(1) - 1)
    def _():
        y = x_ref[...] + gate_ref[...] * acc_ref[...]
        if final_norm:
            ms = jnp.mean(y * y, axis=-1, keepdims=True)
            y = y * lax.rsqrt(ms + NORM_EPS) * fg_ref[...]
        o_ref[...] = y


def mlp_block(x, norm_g, mods, row_fn, w_up, w_down, final_g=None, th=1024, tm=ROW_TILE, name="mlp"):
    m, d = x.shape
    hid = w_up.shape[1]
    tm = min(tm, m)
    row_fn = row_fn(tm)
    final_norm = final_g is not None
    fg = (final_g if final_norm else norm_g).reshape(1, d)
    return pl.pallas_call(
        functools.partial(_mlp_kernel, final_norm=final_norm),
        grid=(m // tm, hid // th),
        in_specs=[
            pl.BlockSpec((tm, d), lambda i, j: (i, 0)),
            pl.BlockSpec((1, d), lambda i, j: (0, 0)),
            _mod_spec(row_fn, 3),
            _mod_spec(row_fn, 4),
            _mod_spec(row_fn, 5),
            pl.BlockSpec((d, th), lambda i, j: (0, j)),
            pl.BlockSpec((th, d), lambda i, j: (j, 0)),
            pl.BlockSpec((1, d), lambda i, j: (0, 0)),
        ],
        out_specs=pl.BlockSpec((tm, d), lambda i, j: (i, 0)),
        out_shape=jax.ShapeDtypeStruct((m, d), F32),
        scratch_shapes=[pltpu.VMEM((tm, d), BF16), pltpu.VMEM((tm, d), F32)],
        compiler_params=_cparams("arbitrary", "arbitrary"),
        name=name,
    )(x, norm_g.reshape(1, d), mods, mods, mods, w_up, w_down, fg)


def rope_tables(head_dim):
    rows = SEQ // GRID_W
    row = jnp.repeat(jnp.arange(rows, dtype=F32), GRID_W)
    col = jnp.tile(jnp.arange(GRID_W, dtype=F32), rows)
    axis_dim = head_dim // 2
    inv_freq = jnp.exp(-math.log(ROPE_BASE) * jnp.arange(0, axis_dim, 2, dtype=F32) / axis_dim)
    ang = jnp.concatenate([row[:, None] * inv_freq, col[:, None] * inv_freq], axis=-1)
    cos = jnp.cos(ang)
    sin = jnp.sin(ang)
    return jnp.concatenate([cos, cos], axis=-1), jnp.concatenate([-sin, sin], axis=-1)


def pair_split_weights(w, qk_width, head_dim):
    d = w.shape[0]
    qk = w[:, :qk_width].reshape(d, qk_width // head_dim, head_dim // 2, 2)
    qk = jnp.swapaxes(qk, 2, 3).reshape(d, qk_width)
    return jnp.concatenate([qk, w[:, qk_width:]], axis=1)


def _rope(x, cos, sin_signed):
    partner = pltpu.roll(x, x.shape[-1] // 2, 1)
    return x * cos + partner * sin_signed


def _ret_tables():
    lg = np.log1p(-np.exp2(-5.0 - np.arange(RET_HEADS, dtype=np.float64)))[:, None]
    pos = np.arange(RET_CHUNK, dtype=np.float64)
    dist = np.abs(pos[:, None] - pos[None, :])
    intra = np.exp(lg[:, :, None] * dist) * (1.0 + np.eye(RET_CHUNK))
    q_f = np.exp(lg * (pos + 1.0))
    q_b = np.exp(lg * (RET_CHUNK - pos))
    k_f = np.exp(lg * (RET_CHUNK - 1.0 - pos))
    k_b = np.exp(lg * pos)
    dec = np.stack([q_f, q_b, k_f, k_b], axis=1)
    dec = np.broadcast_to(dec[..., None], dec.shape + (RET_DK,))
    cdec = np.broadcast_to(np.exp(lg * RET_CHUNK)[:, :, None], (RET_HEADS, 8, RET_DV))
    return (jnp.asarray(intra, F32), jnp.asarray(dec, F32), jnp.asarray(cdec, F32))


def _ret_kernel(ql_ref, kl_ref, vl_ref, gl_ref, qc_ref, kc_ref, vc_ref, gc_ref,
                cos_ref, sin_ref, intra_ref, dec_ref, cdec_ref, gn_ref,
                ol_ref, oc_ref, qs_ref, ks_ref, sb_ref, st_ref):
    C = RET_CHUNK
    n_lat = SEQ // C
    n_ctx = CTX_LEN // C
    kscale = RET_DK ** -0.5
    cd = cdec_ref[0:1, :]

    def rope_chunk(c, carry):
        rows = pl.ds(pl.multiple_of(c * C, C), C)
        cs = cos_ref[rows, :]
        sn = sin_ref[rows, :]
        qs_ref[rows, :] = _rope(ql_ref[rows, :], cs, sn)
        ks_ref[rows, :] = _rope(kl_ref[rows, :], cs, sn) * kscale
        return carry

    lax.fori_loop(0, n_lat, rope_chunk, 0)

    st_ref[...] = jnp.zeros_like(st_ref)

    def back_pair(k1, v1, slot1, k2, v2, slot2):
        st = st_ref[...]
        kv1 = _dot_tn((k1 * dec_ref[3]).astype(BF16), v1)
        kv2 = _dot_tn((k2 * dec_ref[3]).astype(BF16), v2)
        sb_ref[slot1] = st.astype(BF16)
        st = st * cd + kv1
        sb_ref[slot2] = st.astype(BF16)
        st_ref[...] = st * cd + kv2

    assert n_ctx == 2 and n_lat % 2 == 0
    back_pair(kc_ref[pl.ds(C, C), :] * kscale, vc_ref[pl.ds(C, C), :], 1,
              kc_ref[pl.ds(0, C), :] * kscale, vc_ref[pl.ds(0, C), :], 0)

    def back_lat(t, carry):
        c1 = n_lat - 1 - 2 * t
        r1 = pl.ds(pl.multiple_of(c1 * C, C), C)
        r2 = pl.ds(pl.multiple_of((c1 - 1) * C, C), C)
        back_pair(ks_ref[r1, :], vl_ref[r1, :], n_ctx + c1, ks_ref[r2, :], vl_ref[r2, :], n_ctx + c1 - 1)
        return carry

    lax.fori_loop(0, n_lat // 2, back_lat, 0)

    st_ref[...] = jnp.zeros_like(st_ref)

    def norm_gate(o, gate):
        ms = jnp.mean(o * o, axis=-1, keepdims=True)
        on = o * lax.rsqrt(ms + NORM_EPS) * gn_ref[...]
        return (on * gate.astype(F32)).astype(BF16)

    def fwd_pair(q1, k1, v1, gate1, slot1, q2, k2, v2, gate2, slot2):
        st = st_ref[...]
        s1 = _dot_nt(q1.astype(BF16), k1.astype(BF16))
        s2 = _dot_nt(q2.astype(BF16), k2.astype(BF16))
        kv1 = _dot_tn((k1 * dec_ref[2]).astype(BF16), v1)
        cross1 = (_dot((q1 * dec_ref[0]).astype(BF16), st.astype(BF16))
                  + _dot((q1 * dec_ref[1]).astype(BF16), sb_ref[slot1]))
        back2 = _dot((q2 * dec_ref[1]).astype(BF16), sb_ref[slot2])
        kv2 = _dot_tn((k2 * dec_ref[2]).astype(BF16), v2)
        o1 = _dot((s1 * intra_ref[...]).astype(BF16), v1) + cross1
        st = st * cd + kv1
        o2 = (_dot((s2 * intra_ref[...]).astype(BF16), v2) + back2
              + _dot((q2 * dec_ref[0]).astype(BF16), st.astype(BF16)))
        st_ref[...] = st * cd + kv2
        return norm_gate(o1, gate1), norm_gate(o2, gate2)

    r1 = pl.ds(0, C)
    r2 = pl.ds(C, C)
    oc_ref[r1, :], oc_ref[r2, :] = fwd_pair(
        qc_ref[r1, :], kc_ref[r1, :] * kscale, vc_ref[r1, :], gc_ref[r1, :], 0,
        qc_ref[r2, :], kc_ref[r2, :] * kscale, vc_ref[r2, :], gc_ref[r2, :], 1)

    def fwd_lat(t, carry):
        r1 = pl.ds(pl.multiple_of(2 * t * C, C), C)
        r2 = pl.ds(pl.multiple_of((2 * t + 1) * C, C), C)
        ol_ref[r1, :], ol_ref[r2, :] = fwd_pair(
            qs_ref[r1, :], ks_ref[r1, :], vl_ref[r1, :], gl_ref[r1, :], n_ctx + 2 * t,
            qs_ref[r2, :], ks_ref[r2, :], vl_ref[r2, :], gl_ref[r2, :], n_ctx + 2 * t + 1)
        return carry

    lax.fori_loop(0, n_lat // 2, fwd_lat, 0)


def retention_core(qk_lat, v_lat, g_lat, qk_ctx, v_ctx, g_ctx, gn_g, batch):
    H = RET_HEADS
    cos, sin = rope_tables(RET_DK)
    intra, dec, cdec = _ret_tables()
    n_chunks = (SEQ + CTX_LEN) // RET_CHUNK

    def specs(rows):
        return [
            pl.BlockSpec((rows, RET_DK), lambda b, h: (b, h)),
            pl.BlockSpec((rows, RET_DK), lambda b, h: (b, H + h)),
            pl.BlockSpec((rows, RET_DV), lambda b, h: (b, h)),
            pl.BlockSpec((rows, RET_DV), lambda b, h: (b, h)),
        ]

    return pl.pallas_call(
        _ret_kernel,
        grid=(batch, H),
        in_specs=specs(SEQ) + specs(CTX_LEN) + [
            pl.BlockSpec((SEQ, RET_DK), lambda b, h: (0, 0)),
            pl.BlockSpec((SEQ, RET_DK), lambda b, h: (0, 0)),
            pl.BlockSpec((None, RET_CHUNK, RET_CHUNK), lambda b, h: (h, 0, 0)),
            pl.BlockSpec((None, 4, RET_CHUNK, RET_DK), lambda b, h: (h, 0, 0, 0)),
            pl.BlockSpec((None, 8, RET_DV), lambda b, h: (h, 0, 0)),
            pl.BlockSpec((1, RET_DV), lambda b, h: (0, h)),
        ],
        out_specs=[
            pl.BlockSpec((SEQ, RET_DV), lambda b, h: (b, h)),
            pl.BlockSpec((CTX_LEN, RET_DV), lambda b, h: (b, h)),
        ],
        out_shape=[
            jax.ShapeDtypeStruct((batch * SEQ, H * RET_DV), BF16),
            jax.ShapeDtypeStruct((batch * CTX_LEN, H * RET_DV), BF16),
        ],
        scratch_shapes=[
            pltpu.VMEM((SEQ, RET_DK), F32),
            pltpu.VMEM((SEQ, RET_DK), F32),
            pltpu.VMEM((n_chunks, RET_DK, RET_DV), BF16),
            pltpu.VMEM((RET_DK, RET_DV), F32),
        ],
        compiler_params=_cparams("arbitrary", "arbitrary"),
        name="retention_core",
    )(qk_lat, qk_lat, v_lat, g_lat, qk_ctx, qk_ctx, v_ctx, g_ctx,
      cos, sin, intra, dec, cdec, gn_g.reshape(1, H * RET_DV))


def _att_kernel(sink_ref, ql_ref, kl_ref, vl_ref, qc_ref, kc_ref, vc_ref, cos_ref, sin_ref,
                ol_ref, oc_ref, kp_ref, vp_ref):
    BLK = ATT_BLOCK
    G = ATT_GROUP
    HD = ATT_HD
    n_blk = SEQ // BLK
    span = BLK + 2 * WINDOW
    scale = HD ** -0.5
    kvh = pl.program_id(1)

    zpad = jnp.zeros((WINDOW, HD), BF16)
    kp_ref[pl.ds(0, WINDOW), :] = zpad
    vp_ref[pl.ds(0, WINDOW), :] = zpad
    kp_ref[pl.ds(WINDOW + SEQ, WINDOW), :] = zpad
    vp_ref[pl.ds(WINDOW + SEQ, WINDOW), :] = zpad
    ctx0 = SEQ + 2 * WINDOW
    kp_ref[pl.ds(ctx0, CTX_LEN), :] = kc_ref[...].astype(BF16)
    vp_ref[pl.ds(ctx0, CTX_LEN), :] = vc_ref[...]

    def stage(c, carry):
        rows = pl.ds(pl.multiple_of(c * BLK, BLK), BLK)
        dst = pl.ds(pl.multiple_of(c * BLK + WINDOW, BLK), BLK)
        kp_ref[dst, :] = _rope(kl_ref[rows, :], cos_ref[rows, :], sin_ref[rows, :]).astype(BF16)
        vp_ref[dst, :] = vl_ref[rows, :]
        return carry

    lax.fori_loop(0, n_blk, stage, 0)

    kc_b = kp_ref[pl.ds(ctx0, CTX_LEN), :]
    vc_b = vp_ref[pl.ds(ctx0, CTX_LEN), :]

    def finish(parts, out_ref, rows, mask=None):
        pending = []
        for g in range(G):
            sink = jnp.full((BLK, 1), sink_ref[kvh * G + g], F32)
            logits = []
            for i, (s, _) in enumerate(parts):
                sg = s[g * BLK:(g + 1) * BLK, :]
                logits.append(jnp.where(mask, sg, NEG_BIG) if (mask is not None and i == 0) else sg)
            m = sink
            for sg in logits:
                m = jnp.maximum(m, jnp.max(sg, axis=-1, keepdims=True))
            den = jnp.exp(sink - m)
            acc = None
            for sg, (_, v) in zip(logits, parts):
                p = jnp.exp(sg - m)
                den = den + jnp.sum(p, axis=-1, keepdims=True)
                pv = _dot(p.astype(BF16), v)
                acc = pv if acc is None else acc + pv
            pending.append((acc, den))
        for g, (acc, den) in enumerate(pending):
            out_ref[rows, pl.ds(g * HD, HD)] = (acc / den).astype(BF16)

    qi = lax.broadcasted_iota(jnp.int32, (BLK, span), 0)
    col = lax.broadcasted_iota(jnp.int32, (BLK, span), 1)
    band = (col >= qi) & (col <= qi + 2 * WINDOW)

    def lat_block(bi, carry):
        rows = pl.ds(pl.multiple_of(bi * BLK, BLK), BLK)
        cs = cos_ref[rows, :]
        sn = sin_ref[rows, :]
        qs = [(_rope(ql_ref[rows, pl.ds(g * HD, HD)], cs, sn) * scale).astype(BF16) for g in range(G)]
        q4 = jnp.concatenate(qs, axis=0)
        krows = pl.ds(pl.multiple_of(bi * BLK, BLK), span)
        s_lat = _dot_nt(q4, kp_ref[krows, :])
        kpos = col + (bi * BLK - WINDOW)
        valid = band & (kpos >= 0) & (kpos < SEQ)
        s_ctx = _dot_nt(q4, kc_b)
        finish([(s_lat, vp_ref[krows, :]), (s_ctx, vc_b)], ol_ref, rows, mask=valid)
        return carry

    lax.fori_loop(0, n_blk, lat_block, 0)

    for bi in range(CTX_LEN // BLK):
        rows = pl.ds(bi * BLK, BLK)
        qs = [(qc_ref[rows, pl.ds(g * HD, HD)] * scale).astype(BF16) for g in range(G)]
        q4 = jnp.concatenate(qs, axis=0)
        finish([(_dot_nt(q4, kc_b), vc_b)], oc_ref, rows)


def attention_core(q_lat, k_lat, v_lat, q_ctx, k_ctx, v_ctx, sink, batch):
    cos, sin = rope_tables(ATT_HD)
    GW = ATT_GROUP * ATT_HD
    pad_rows = SEQ + 2 * WINDOW + CTX_LEN

    def specs(rows):
        return [
            pl.BlockSpec((rows, GW), lambda b, h: (b, h)),
            pl.BlockSpec((rows, ATT_HD), lambda b, h: (b, h)),
            pl.BlockSpec((rows, ATT_HD), lambda b, h: (b, h)),
        ]

    return pl.pallas_call(
        _att_kernel,
        grid=(batch, ATT_KV_HEADS),
        in_specs=[pl.BlockSpec(memory_space=pltpu.SMEM)] + specs(SEQ) + specs(CTX_LEN) + [
            pl.BlockSpec((SEQ, ATT_HD), lambda b, h: (0, 0)),
            pl.BlockSpec((SEQ, ATT_HD), lambda b, h: (0, 0)),
        ],
        out_specs=[
            pl.BlockSpec((SEQ, GW), lambda b, h: (b, h)),
            pl.BlockSpec((CTX_LEN, GW), lambda b, h: (b, h)),
        ],
        out_shape=[
            jax.ShapeDtypeStruct((batch * SEQ, ATT_HEADS * ATT_HD), BF16),
            jax.ShapeDtypeStruct((batch * CTX_LEN, ATT_HEADS * ATT_HD), BF16),
        ],
        scratch_shapes=[
            pltpu.VMEM((pad_rows, ATT_HD), BF16),
            pltpu.VMEM((pad_rows, ATT_HD), BF16),
        ],
        compiler_params=_cparams("arbitrary", "arbitrary"),
        name="attention_core",
    )(sink.astype(F32), q_lat, k_lat, v_lat, q_ctx, k_ctx, v_ctx, cos, sin)


def _sgu_kernel(u_ref, v_ref, g_ref, b_ref, ws_ref, bs_ref, o_ref, *, chunks):
    C = SGU_CHUNK
    gw = SGU_WIDTH // SGU_GROUPS
    for c in range(chunks):
        rows = pl.ds(c * C, C)
        v = v_ref[rows, :]
        mu = jnp.mean(v, axis=-1, keepdims=True)
        xc = v - mu
        var = jnp.mean(xc * xc, axis=-1, keepdims=True)
        vn = (xc * lax.rsqrt(var + NORM_EPS) * g_ref[...] + b_ref[...]).astype(BF16)
        for g in range(SGU_GROUPS):
            cols = pl.ds(g * gw, gw)
            t = _dot(ws_ref[g], vn[:, g * gw:(g + 1) * gw]) + bs_ref[g]
            o_ref[rows, cols] = (u_ref[rows, cols].astype(F32) * t).astype(BF16)


def sgu_core(u, v, ln_g, ln_b, w_s, b_s, chunks=2):
    m = u.shape[0]
    tm = chunks * SGU_CHUNK
    return pl.pallas_call(
        functools.partial(_sgu_kernel, chunks=chunks),
        grid=(m // tm,),
        in_specs=[
            pl.BlockSpec((tm, SGU_WIDTH), lambda i: (i, 0)),
            pl.BlockSpec((tm, SGU_WIDTH), lambda i: (i, 0)),
            pl.BlockSpec((1, SGU_WIDTH), lambda i: (0, 0)),
            pl.BlockSpec((1, SGU_WIDTH), lambda i: (0, 0)),
            pl.BlockSpec((SGU_GROUPS, SGU_CHUNK, SGU_CHUNK), lambda i: (0, 0, 0)),
            pl.BlockSpec((SGU_GROUPS, SGU_CHUNK, 1), lambda i: (0, 0, 0)),
        ],
        out_specs=pl.BlockSpec((tm, SGU_WIDTH), lambda i: (i, 0)),
        out_shape=jax.ShapeDtypeStruct((m, SGU_WIDTH), BF16),
        compiler_params=_cparams("arbitrary"),
        name="sgu_core",
    )(u, v, ln_g.reshape(1, -1), ln_b.reshape(1, -1), w_s.astype(BF16),
      b_s.reshape(SGU_GROUPS, SGU_CHUNK, 1))


GDN_ROWS = CTX_LEN + SEQ
GDN_NCHUNK = GDN_ROWS // GDN_CHUNK
GDN_CONV_ROWS = 256
GDN_HALO = 8
GDN_PREP_TILES = 3


def _gdn_kernel(ql_ref, kl_ref, vl_ref, zl_ref, qc_ref, kc_ref, vc_ref, ba_ref,
                cwq_ref, cwk_ref, cwv_ref, alog_ref, dt_ref, ng_ref,
                ol_ref,
                xp_ref, qn_ref, kn_ref, vv_ref, gt_ref, qp_ref, g_ref, b_ref,
                el_ref, of_ref, ob_ref, s_ref):
    C = GDN_CHUNK
    DK = GDN_DK
    R = GDN_CONV_ROWS
    HALO = GDN_HALO

    def conv_silu(src_ref, n_rows, cw_ref, width, store):
        cols = pl.ds(0, width)
        xp_ref[pl.ds(0, HALO), cols] = jnp.zeros((HALO, width), F32)
        xp_ref[pl.ds(HALO, n_rows), cols] = src_ref[...]
        xp_ref[pl.ds(HALO + n_rows, HALO), cols] = jnp.zeros((HALO, width), F32)
        w = cw_ref[...]
        n = R + 2 * HALO

        def body(c, carry):
            r0 = pl.multiple_of(c * R, R)
            x = xp_ref[pl.ds(r0, n), cols]
            y = (w[2:3] * x + w[1:2] * pltpu.roll(x, 1, 0) + w[0:1] * pltpu.roll(x, 2, 0)
                 + w[3:4] * pltpu.roll(x, n - 1, 0))[HALO:HALO + R]
            store(r0, _silu(y))
            return carry

        lax.fori_loop(0, n_rows // R, body, 0)

    def unit(y):
        return y * lax.rsqrt(jnp.sum(y * y, axis=-1, keepdims=True) + 1e-6)

    def store_q(off):
        def f(r0, y):
            qn_ref[pl.ds(pl.multiple_of(off + r0, R), R), :] = unit(y) * (DK ** -0.5)
        return f

    def store_k(off):
        def f(r0, y):
            kn_ref[pl.ds(pl.multiple_of(off + r0, R), R), :] = unit(y)
        return f

    def store_v(off):
        def f(r0, y):
            vv_ref[pl.ds(pl.multiple_of(off + r0, R), R), :] = y
        return f

    conv_silu(qc_ref, CTX_LEN, cwq_ref, DK, store_q(0))
    conv_silu(kc_ref, CTX_LEN, cwk_ref, DK, store_k(0))
    conv_silu(vc_ref, CTX_LEN, cwv_ref, 2 * GDN_DV, store_v(0))
    conv_silu(ql_ref, SEQ, cwq_ref, DK, store_q(CTX_LEN))
    conv_silu(kl_ref, SEQ, cwk_ref, DK, store_k(CTX_LEN))
    conv_silu(vl_ref, SEQ, cwv_ref, 2 * GDN_DV, store_v(CTX_LEN))

    raw = ba_ref[...]
    xg = raw + dt_ref[...]
    softplus = jnp.maximum(xg, 0.0) + jnp.log1p(jnp.exp(-jnp.abs(xg)))
    g = -jnp.exp(alog_ref[...]) * softplus
    pos = lax.broadcasted_iota(jnp.int32, raw.shape, 1) % C
    pre = g
    suf = g
    shift = 1
    while shift < C:
        pre = pre + jnp.where(pos >= shift, pltpu.roll(pre, shift, 1), 0.0)
        suf = suf + jnp.where(pos + shift < C, pltpu.roll(suf, GDN_ROWS - shift, 1), 0.0)
        shift *= 2
    row = lax.broadcasted_iota(jnp.int32, raw.shape, 0)
    gt_ref[...] = jnp.where(row < 4, jax.nn.sigmoid(raw), jnp.where(row < 6, pre, suf))

    PK = 2 * C
    ri = lax.broadcasted_iota(jnp.int32, (C, PK), 0)
    li = lax.broadcasted_iota(jnp.int32, (C, PK), 1)
    ci = li % C
    left = li < C
    eye = (ri == ci).astype(F32)
    incl = (ri >= ci, ri <= ci)
    strict = (ri > ci, ri < ci)
    half_sel = (li == ri, li == ri + C)

    def to_col(rowvec, half):
        return jnp.sum(jnp.where(half_sel[half], rowvec, 0.0), axis=1, keepdims=True)

    def block_diag(m, zero):
        return jnp.concatenate([jnp.concatenate([m[0], zero], axis=1),
                                jnp.concatenate([zero, m[1]], axis=1)], axis=0)

    def dot3_packed(x, m):
        bd = jnp.concatenate([jnp.where(left, m, 0.0), jnp.where(left, 0.0, m)], axis=0)
        x_hi = x.astype(BF16)
        bd_hi = bd.astype(BF16)
        x_lo = (x - x_hi.astype(F32)).astype(BF16)
        bd_lo = (bd - bd_hi.astype(F32)).astype(BF16)
        rhs = jnp.concatenate([jnp.concatenate([bd_hi, bd_lo], axis=1),
                               jnp.concatenate([bd_hi, jnp.zeros_like(bd_hi)], axis=1)], axis=0)
        out = _dot(jnp.concatenate([x_hi, x_lo], axis=1), rhs)
        return out[:, :PK] + out[:, PK:]

    def prep_pair(p, carry):
        zero_uw = jnp.zeros((C, 2 * GDN_DV), BF16)
        chains = []
        for sub in range(2 * GDN_PREP_TILES):
            tile, half = divmod(sub, 2)
            if half == 0:
                lanes = pl.ds(pl.multiple_of((p * GDN_PREP_TILES + tile) * PK, PK), PK)
                gates = gt_ref[:, lanes]
                gates_sw = pltpu.roll(gates, C, 1)
            chunk = (p * GDN_PREP_TILES + tile) * 2 + half
            rows = pl.ds(pl.multiple_of(chunk * C, C), C)
            q = qn_ref[rows, :]
            k = kn_ref[rows, :]
            v2 = vv_ref[rows, :]
            qk_kk = _dot_nt(jnp.concatenate([q, k], axis=0).astype(BF16),
                            jnp.concatenate([k, k], axis=0).astype(BF16))
            for d in range(2):
                def packed_row(r, half=half, d=d):
                    a = (gates if half == 0 else gates_sw)[r + 2 * d:r + 2 * d + 1]
                    b = (gates_sw if half == 0 else gates)[r + 2 * d + 1:r + 2 * d + 2]
                    return jnp.where(left[0:1], a, b)
                edge = half * C + (C - 1 if d == 0 else 0)
                chains.append(dict(
                    d=d, n=chunk, rows=rows, q=q, k=k, v2=v2, qk=qk_kk[:C], kk=qk_kk[C:],
                    grow=packed_row(4),
                    bcols=[to_col(gates[2 * d + j:2 * d + j + 1], half) for j in range(2)],
                    gcols=[to_col(gates[4 + 2 * d + j:5 + 2 * d + j], half) for j in range(2)],
                    glast=[gates[4 + 2 * d + j:5 + 2 * d + j, edge:edge + 1] for j in range(2)]))
        for ch in chains:
            d = ch["d"]
            bcol = jnp.where(left, ch["bcols"][0], ch["bcols"][1])
            gcol = jnp.where(left, ch["gcols"][0], ch["gcols"][1])
            ch["dec"] = jnp.exp(jnp.where(incl[d], gcol - ch["grow"], NEG_BIG))
            nmat = jnp.where(strict[d], -(ch["kk"] * bcol * ch["dec"]), 0.0)
            ch["pm"] = eye + nmat
            ch["nmat"] = nmat
        for ch in chains:
            ch["mm"] = dot3_packed(ch["nmat"], ch["nmat"])
        for _ in range(4):
            for ch in chains:
                r = dot3_packed(jnp.concatenate([ch["pm"], ch["mm"]], axis=0), ch["mm"])
                ch["pm"] = ch["pm"] + r[:C]
                ch["mm"] = r[C:]
        for ch in chains:
            ch["r"] = dot3_packed(ch["pm"], ch["mm"])
        for ch in chains:
            tmat = (ch["pm"] + ch["r"]).astype(BF16)
            ch["eg"] = [jnp.exp(g) for g in ch["gcols"]]
            rhs = [jnp.concatenate([ch["v2"][:, j * GDN_DV:(j + 1) * GDN_DV] * ch["bcols"][j],
                                    ch["k"] * (ch["bcols"][j] * ch["eg"][j])], axis=1).astype(BF16)
                   for j in range(2)]
            ch["uw"] = _dot(tmat, block_diag(rhs, zero_uw)).astype(BF16)
        for ch in chains:
            uw = [ch["uw"][:, j * 2 * GDN_DV:(j + 1) * 2 * GDN_DV] for j in range(2)]
            amat = (ch["qk"] * ch["dec"]).astype(BF16)
            ch["a_uw"] = _dot(amat, block_diag(uw, zero_uw))
            ch["k_uw"] = [_dot_tn((ch["k"] * jnp.exp(ch["glast"][j] - ch["gcols"][j])).astype(BF16), uw[j])
                          for j in range(2)]
        for ch in chains:
            n, rows = ch["n"], ch["rows"]
            out_ref = of_ref if ch["d"] == 0 else ob_ref
            for j in range(2):
                combo = 2 * ch["d"] + j
                a_u = ch["a_uw"][:, (2 * j) * GDN_DV:(2 * j + 1) * GDN_DV]
                a_w = ch["a_uw"][:, (2 * j + 1) * GDN_DV:(2 * j + 2) * GDN_DV]
                qp_ref[combo, rows, :] = (ch["q"] * ch["eg"][j] - a_w).astype(BF16)
                out_ref[rows, pl.ds(j * GDN_DV, GDN_DV)] = a_u
                b_ref[combo, n] = ch["k_uw"][j][:, :GDN_DV]
                g_ref[combo, n] = ch["k_uw"][j][:, GDN_DV:].astype(BF16)
                el_ref[combo, n] = jnp.broadcast_to(jnp.exp(ch["glast"][j]), (8, GDN_DV))
        return carry

    lax.fori_loop(0, GDN_NCHUNK // (2 * GDN_PREP_TILES), prep_pair, 0)

    s_ref[...] = jnp.zeros_like(s_ref)

    def rec_pair(nf, nb):
        work = []
        for combo in range(4):
            n = nf if combo < 2 else nb
            rows = pl.ds(pl.multiple_of(n * C, C), C)
            out_ref = of_ref if combo < 2 else ob_ref
            cols = pl.ds((combo % 2) * GDN_DV, GDN_DV)
            lhs = jnp.concatenate([g_ref[combo, n], qp_ref[combo, rows, :]], axis=0)
            work.append((combo, out_ref, rows, cols, lhs, s_ref[combo], b_ref[combo, n],
                         el_ref[combo, n][0:1, :], out_ref[rows, cols]))
        done = []
        for combo, out_ref, rows, cols, lhs, st, bmat, e_last, o0 in work:
            r = _dot(lhs, st.astype(BF16))
            done.append((combo, out_ref, rows, cols, st * e_last + bmat - r[:GDN_DK], o0 + r[GDN_DK:]))
        for combo, out_ref, rows, cols, st_new, o in done:
            s_ref[combo] = st_new
            out_ref[rows, cols] = o

    n_ctx = CTX_LEN // C
    for t in range(n_ctx):
        rec_pair(t, n_ctx - 1 - t)

    def rec_lat(t, carry):
        rec_pair(n_ctx + t, GDN_NCHUNK - 1 - t)
        return carry

    lax.fori_loop(0, SEQ // C, rec_lat, 0)

    def fin(c, carry):
        src = pl.ds(pl.multiple_of(CTX_LEN + c * R, R), R)
        dst = pl.ds(pl.multiple_of(c * R, R), R)
        for j in range(2):
            cols = pl.ds(j * GDN_DV, GDN_DV)
            o = of_ref[src, cols] + ob_ref[src, cols]
            ms = jnp.mean(o * o, axis=-1, keepdims=True)
            on = o * lax.rsqrt(ms + NORM_EPS) * ng_ref[...]
            ol_ref[dst, cols] = (on * zl_ref[dst, cols].astype(F32)).astype(BF16)
        return carry

    lax.fori_loop(0, SEQ // R, fin, 0)


def gdn_core(mixed_lat, z_lat, mixed_ctx, ba, conv_w, a_log, dt_bias, norm_g, batch):
    HK = GDN_K_HEADS
    DK = GDN_DK
    VW = 2 * GDN_DV
    v_blk0 = (2 * HK * DK) // VW

    def tab(t):
        t = t.reshape(2, HK, 2).transpose(1, 0, 2).reshape(HK, 4)
        return jnp.concatenate([jnp.zeros_like(t), t], axis=1).reshape(HK, 8, 1).astype(F32)

    def specs(rows):
        return [
            pl.BlockSpec((rows, DK), lambda b, h: (b, h)),
            pl.BlockSpec((rows, DK), lambda b, h: (b, HK + h)),
            pl.BlockSpec((rows, VW), lambda b, h: (b, v_blk0 + h)),
        ]

    nc = GDN_NCHUNK
    return pl.pallas_call(
        _gdn_kernel,
        grid=(batch, HK),
        in_specs=specs(SEQ) + [pl.BlockSpec((SEQ, VW), lambda b, h: (b, h))] + specs(CTX_LEN) + [
            pl.BlockSpec((None, None, 8, GDN_ROWS), lambda b, h: (b, h, 0, 0)),
            pl.BlockSpec((GDN_CONV, DK), lambda b, h: (0, h)),
            pl.BlockSpec((GDN_CONV, DK), lambda b, h: (0, HK + h)),
            pl.BlockSpec((GDN_CONV, VW), lambda b, h: (0, v_blk0 + h)),
            pl.BlockSpec((None, 8, 1), lambda b, h: (h, 0, 0)),
            pl.BlockSpec((None, 8, 1), lambda b, h: (h, 0, 0)),
            pl.BlockSpec((1, GDN_DV), lambda b, h: (0, 0)),
        ],
        out_specs=pl.BlockSpec((SEQ, VW), lambda b, h: (b, h)),
        out_shape=jax.ShapeDtypeStruct((batch * SEQ, GDN_V_HEADS * GDN_DV), BF16),
        scratch_shapes=[
            pltpu.VMEM((SEQ + 2 * GDN_HALO, VW), F32),
            pltpu.VMEM((GDN_ROWS, DK), F32),
            pltpu.VMEM((GDN_ROWS, DK), F32),
            pltpu.VMEM((GDN_ROWS, VW), F32),
            pltpu.VMEM((8, GDN_ROWS), F32),
            pltpu.VMEM((4, GDN_ROWS, DK), BF16),
            pltpu.VMEM((4, nc, DK, GDN_DV), BF16),
            pltpu.VMEM((4, nc, DK, GDN_DV), F32),
            pltpu.VMEM((4, nc, 8, GDN_DV), F32),
            pltpu.VMEM((GDN_ROWS, VW), F32),
            pltpu.VMEM((GDN_ROWS, VW), F32),
            pltpu.VMEM((4, DK, GDN_DV), F32),
        ],
        compiler_params=_cparams("arbitrary", "arbitrary"),
        name="gdn_core",
    )(mixed_lat, mixed_lat, mixed_lat, z_lat, mixed_ctx, mixed_ctx, mixed_ctx, ba,
      conv_w, conv_w, conv_w, tab(a_log), tab(dt_bias), norm_g.reshape(1, GDN_DV))


def _finish_layer(o_lat, o_ctx, xl, xc, w_out, norm2_g, up, down, mods, rows, final_g=None):
    lat_row, ctx_row = rows
    w_out = w_out.astype(BF16)
    up = up.astype(BF16)
    down = down.astype(BF16)
    xl = linear_post(o_lat, w_out, xl, mods, lat_row, 2, name="out_proj_lat")
    xl = mlp_block(xl, norm2_g, mods, lat_row, up, down, final_g=final_g, name="mlp_lat")
    if o_ctx is not None:
        xc = linear_post(o_ctx, w_out, xc, mods, ctx_row, 2, name="out_proj_ctx")
        xc = mlp_block(xc, norm2_g, mods, ctx_row, up, down, name="mlp_ctx")
    return xl, xc


def retention_layer(xl, xc, mods, rows, batch, norm1_g, norm2_g, w_in, gn_g, w_out, up, down):
    lat_row, ctx_row = rows
    qkw = 2 * RET_HEADS * RET_DK
    vw = RET_HEADS * RET_DV
    w_in = pair_split_weights(w_in.astype(BF16), qkw, RET_DK)
    segs = [(qkw, F32, _identity), (vw, BF16, _identity), (vw, BF16, _silu)]
    qk_l, v_l, g_l = linear_pre(xl, norm1_g, mods, lat_row, w_in, segs, name="ret_in_lat")
    qk_c, v_c, g_c = linear_pre(xc, norm1_g, mods, ctx_row, w_in, segs, name="ret_in_ctx")
    o_l, o_c = retention_core(qk_l, v_l, g_l, qk_c, v_c, g_c, gn_g, batch)
    return _finish_layer(o_l, o_c, xl, xc, w_out, norm2_g, up, down, mods, rows)


def attention_layer(xl, xc, mods, rows, batch, norm1_g, norm2_g, w_in, sink, w_out, up, down):
    lat_row, ctx_row = rows
    qw = ATT_HEADS * ATT_HD
    kvw = ATT_KV_HEADS * ATT_HD
    w_in = pair_split_weights(w_in.astype(BF16), qw + kvw, ATT_HD)
    segs = [(qw, F32, _identity), (kvw, F32, _identity), (kvw, BF16, _identity)]
    q_l, k_l, v_l = linear_pre(xl, norm1_g, mods, lat_row, w_in, segs, name="att_in_lat")
    q_c, k_c, v_c = linear_pre(xc, norm1_g, mods, ctx_row, w_in, segs, name="att_in_ctx")
    o_l, o_c = attention_core(q_l, k_l, v_l, q_c, k_c, v_c, sink, batch)
    return _finish_layer(o_l, o_c, xl, xc, w_out, norm2_g, up, down, mods, rows)


def gdn_layer(xl, xc, mods, rows, batch, norm1_g, norm2_g, w_in, conv_w, a_log, dt_bias, norm_g,
              w_out, up, down):
    lat_row, ctx_row = rows
    conv_ch = 2 * GDN_K_HEADS * GDN_DK + GDN_V_HEADS * GDN_DV
    zw = GDN_V_HEADS * GDN_DV
    w_main = w_in[:, :conv_ch + zw].astype(BF16)
    w_gate = w_in[:, conv_ch + zw:].astype(BF16)
    segs = [(conv_ch, F32, _identity), (zw, BF16, _silu)]
    gsegs = [(4 * GDN_V_HEADS, F32, _identity)]
    gate_w = 4 * GDN_V_HEADS
    mx_l, z_l = linear_pre(xl, norm1_g, mods, lat_row, w_main, segs, name="gdn_in_lat")
    mx_c, _ = linear_pre(xc, norm1_g, mods, ctx_row, w_main, segs, name="gdn_in_ctx")
    (ba_l,) = linear_pre(xl, norm1_g, mods, lat_row, w_gate, gsegs, tn=gate_w, name="gdn_gate_lat")
    (ba_c,) = linear_pre(xc, norm1_g, mods, ctx_row, w_gate, gsegs, tn=gate_w, name="gdn_gate_ctx")
    ba = jnp.concatenate([ba_c.reshape(batch, CTX_LEN, -1), ba_l.reshape(batch, SEQ, -1)], axis=1)
    ba = ba.reshape(batch, GDN_ROWS, 2, 2, GDN_K_HEADS, 2).transpose(0, 4, 2, 3, 5, 1)
    ba = ba.reshape(batch, GDN_K_HEADS, 8, GDN_ROWS)
    o_l = gdn_core(mx_l, z_l, mx_c, ba, conv_w, a_log, dt_bias, norm_g, batch)
    return _finish_layer(o_l, None, xl, xc, w_out, norm2_g, up, down, mods, rows)


def sgu_layer(xl, xc, mods, rows, norm1_g, norm2_g, w_in, ln_g, ln_b, w_s, b_s, w_out, up, down,
              final_g):
    lat_row, _ = rows
    segs = [(SGU_WIDTH, BF16, _gelu_erf), (SGU_WIDTH, F32, _gelu_erf)]
    u_l, v_l = linear_pre(xl, norm1_g, mods, lat_row, w_in.astype(BF16), segs, name="sgu_in_lat")
    o_l = sgu_core(u_l, v_l, ln_g, ln_b, w_s, b_s)
    return _finish_layer(o_l, None, xl, xc, w_out, norm2_g, up, down, mods, rows, final_g=final_g)


def kernel(x, c, ctx, c_ctx, mod_w, mod_b, norm1_g, norm2_g, mlp_up, mlp_down, final_g,
           ret_w_in, ret_gn_g, ret_w_out, att_w_in, att_sink, att_w_out,
           gdn_w_in, gdn_conv_w, gdn_a_log, gdn_dt_bias, gdn_norm_g, gdn_w_out,
           sgu_w_in, sgu_ln_g, sgu_ln_b, sgu_w_s, sgu_b_s, sgu_w_out):
    batch, seq, d = x.shape
    assert (seq, d, ctx.shape[1]) == (SEQ, D_MODEL, CTX_LEN) and batch < MOD_ROWS
    xl = x.reshape(batch * SEQ, d)
    xc = ctx.reshape(batch * CTX_LEN, d)

    cond = jnp.zeros((MOD_ROWS, d), F32).at[:batch].set(c).at[batch].set(c_ctx)
    mods_all = ada_mods(cond, mod_w, mod_b).reshape(DEPTH, MOD_ROWS, 1, 6 * d)
    rows = (functools.partial(_row_fn, SEQ, False, batch),
            functools.partial(_row_fn, CTX_LEN, True, batch))

    xl, xc = retention_layer(xl, xc, mods_all[0], rows, batch, norm1_g[0], norm2_g[0],
                             ret_w_in[0], ret_gn_g[0], ret_w_out[0], mlp_up[0], mlp_down[0])
    xl, xc = attention_layer(xl, xc, mods_all[1], rows, batch, norm1_g[1], norm2_g[1],
                             att_w_in[0], att_sink[0], att_w_out[0], mlp_up[1], mlp_down[1])
    xl, xc = gdn_layer(xl, xc, mods_all[2], rows, batch, norm1_g[2], norm2_g[2],
                       gdn_w_in[0], gdn_conv_w[0], gdn_a_log[0], gdn_dt_bias[0], gdn_norm_g[0],
                       gdn_w_out[0], mlp_up[2], mlp_down[2])
    xl, xc = sgu_layer(xl, xc, mods_all[3], rows, norm1_g[3], norm2_g[3],
                       sgu_w_in[0], sgu_ln_g[0], sgu_ln_b[0], sgu_w_s[0], sgu_b_s[0], sgu_w_out[0],
                       mlp_up[3], mlp_down[3], final_g)
    return xl.reshape(batch, SEQ, d)
```

```python
import functools
import math

import numpy as np
import jax
import jax.numpy as jnp
from jax import lax
from jax.experimental import pallas as pl
from jax.experimental.pallas import tpu as pltpu

F32 = jnp.float32
BF16 = jnp.bfloat16

D_MODEL = 2048
SEQ = 2048
CTX_LEN = 256
GRID_W = 64
DEPTH = 4
NORM_EPS = 1e-6
ROPE_BASE = 10000.0
MLP_HIDDEN = 4 * D_MODEL

RET_HEADS = 8
RET_DK = 256
RET_DV = 512
RET_CHUNK = 128

ATT_HEADS = 16
ATT_KV_HEADS = 4
ATT_HD = 128
ATT_GROUP = 4
WINDOW = 128
ATT_BLOCK = 128

GDN_K_HEADS = 16
GDN_V_HEADS = 32
GDN_DK = 128
GDN_DV = 128
GDN_CONV = 4
GDN_CHUNK = 64

SGU_CHUNK = 128
SGU_GROUPS = 8
SGU_WIDTH = 2 * D_MODEL

VMEM_LIMIT_BYTES = 56 * 1024 * 1024
MOD_ROWS = 16
ROW_TILE = 512
PRE_ROW_TILE = 1024
PRE_COL_TILE = 512
NEG_BIG = -1e30


def _cparams(*sem):
    return pltpu.CompilerParams(dimension_semantics=sem, vmem_limit_bytes=VMEM_LIMIT_BYTES)


def _silu(x):
    return (0.5 * x) * (1.0 + jnp.tanh(0.5 * x))


def _identity(x):
    return x


def _gelu_erf(x):
    return 0.5 * x * (1.0 + lax.erf(x * (2.0 ** -0.5)))


def _dot(a, b):
    return jnp.dot(a, b, preferred_element_type=F32)


def _dot_nt(a, b):
    return lax.dot_general(a, b, (((1,), (1,)), ((), ())), preferred_element_type=F32)


def _dot_tn(a, b):
    return lax.dot_general(a, b, (((0,), (0,)), ((), ())), preferred_element_type=F32)


def _mods_kernel(cond_ref, w_ref, b_ref, o_ref):
    a = _silu(cond_ref[...]).astype(BF16)
    o_ref[...] = _dot(a, w_ref[...].astype(BF16)) + b_ref[...]


def ada_mods(cond, mod_w, mod_b):
    depth, d, n = mod_w.shape
    tn = 1024
    return pl.pallas_call(
        _mods_kernel,
        grid=(depth, n // tn),
        in_specs=[
            pl.BlockSpec((MOD_ROWS, d), lambda l, j: (0, 0)),
            pl.BlockSpec((None, d, tn), lambda l, j: (l, 0, j)),
            pl.BlockSpec((None, 1, tn), lambda l, j: (l, 0, j)),
        ],
        out_specs=pl.BlockSpec((None, MOD_ROWS, tn), lambda l, j: (l, 0, j)),
        out_shape=jax.ShapeDtypeStruct((depth, MOD_ROWS, n), F32),
        compiler_params=_cparams("arbitrary", "arbitrary"),
        name="ada_mods",
    )(cond, mod_w, mod_b.reshape(depth, 1, n))


def _mod_spec(row_fn, chunk, tn=None):
    if tn is None:
        return pl.BlockSpec((None, 1, D_MODEL), lambda i, j: (row_fn(i), 0, chunk))
    per = D_MODEL // tn
    return pl.BlockSpec((None, 1, tn), lambda i, j: (row_fn(i), 0, chunk * per + j))


def _row_fn(rows_per_batch, is_ctx, batch, tm):
    if is_ctx:
        return lambda i: batch
    per = rows_per_batch // tm
    return lambda i: i // per


NORM_ROWS = 16


def _norm_mod_into(x_ref, g_ref, sh_ref, sc_ref, h_ref):
    g = g_ref[...]
    shift = sh_ref[...]
    gain = 1.0 + sc_ref[...]

    def body(c, carry):
        rows = pl.ds(pl.multiple_of(c * NORM_ROWS, NORM_ROWS), NORM_ROWS)
        x = x_ref[rows, :]
        ms = jnp.mean(x * x, axis=-1, keepdims=True)
        h_ref[rows, :] = ((x * lax.rsqrt(ms + NORM_EPS) * g) * gain + shift).astype(BF16)
        return carry

    lax.fori_loop(0, x_ref.shape[0] // NORM_ROWS, body, 0, unroll=8)


def _linear_pre_kernel(x_ref, g_ref, sh_ref, sc_ref, w_ref, *rest, segs):
    out_refs = rest[:len(segs)]
    h_ref = rest[len(segs)]
    j = pl.program_id(1)

    @pl.when(j == 0)
    def _():
        _norm_mod_into(x_ref, g_ref, sh_ref, sc_ref, h_ref)

    start = 0
    for (ntiles, _, act), o_ref in zip(segs, out_refs):
        @pl.when((j >= start) & (j < start + ntiles))
        def _(o_ref=o_ref, act=act):
            o_ref[...] = act(_dot(h_ref[...], w_ref[...])).astype(o_ref.dtype)
        start += ntiles


def linear_pre(x, norm_g, mods, row_fn, w, segments, tn=PRE_COL_TILE, tm=PRE_ROW_TILE, col0=0,
               name="linear_pre"):
    m, d = x.shape
    n = sum(width for width, _, _ in segments)
    assert col0 % tn == 0 and col0 + n <= w.shape[1]
    tm = min(tm, m)
    row_fn = row_fn(tm)
    segs = []
    out_specs = []
    out_shapes = []
    start = 0
    for width, dtype, act in segments:
        nt = width // tn
        assert nt * tn == width
        segs.append((nt, dtype, act))
        out_specs.append(pl.BlockSpec(
            (tm, tn), lambda i, j, s=start, nt=nt: (i, jnp.clip(j - s, 0, nt - 1))))
        out_shapes.append(jax.ShapeDtypeStruct((m, width), dtype))
        start += nt
    assert start * tn == n
    outs = pl.pallas_call(
        functools.partial(_linear_pre_kernel, segs=tuple(segs)),
        grid=(m // tm, n // tn),
        in_specs=[
            pl.BlockSpec((tm, d), lambda i, j: (i, 0)),
            pl.BlockSpec((1, d), lambda i, j: (0, 0)),
            _mod_spec(row_fn, 0),
            _mod_spec(row_fn, 1),
            pl.BlockSpec((d, tn), lambda i, j: (0, col0 // tn + j)),
        ],
        out_specs=out_specs,
        out_shape=out_shapes,
        scratch_shapes=[pltpu.VMEM((tm, d), BF16)],
        compiler_params=_cparams("arbitrary", "arbitrary"),
        name=name,
    )(x, norm_g.reshape(1, d), mods, mods, w)
    return outs


def _linear_post_kernel(a_ref, w_ref, res_ref, gate_ref, o_ref):
    o_ref[...] = res_ref[...] + gate_ref[...] * _dot(a_ref[...], w_ref[...])


def linear_post(a, w, res, mods, row_fn, gate_chunk, tn=PRE_COL_TILE, tm=PRE_ROW_TILE, name="linear_post"):
    m, k = a.shape
    n = w.shape[1]
    tm = min(tm, m)
    row_fn = row_fn(tm)
    return pl.pallas_call(
        _linear_post_kernel,
        grid=(m // tm, n // tn),
        in_specs=[
            pl.BlockSpec((tm, k), lambda i, j: (i, 0)),
            pl.BlockSpec((k, tn), lambda i, j: (0, j)),
            pl.BlockSpec((tm, tn), lambda i, j: (i, j)),
            _mod_spec(row_fn, gate_chunk, tn),
        ],
        out_specs=pl.BlockSpec((tm, tn), lambda i, j: (i, j)),
        out_shape=jax.ShapeDtypeStruct((m, n), F32),
        compiler_params=_cparams("arbitrary", "arbitrary"),
        name=name,
    )(a, w, res, mods)


def _mlp_kernel(x_ref, g_ref, sh_ref, sc_ref, gate_ref, wu_ref, wd_ref, fg_ref, o_ref,
                h_ref, acc_ref, *, final_norm):
    j = pl.program_id(1)

    @pl.when(j == 0)
    def _():
        _norm_mod_into(x_ref, g_ref, sh_ref, sc_ref, h_ref)

    def hidden_step():
        a = jnp.maximum(_dot(h_ref[...], wu_ref[...]), 0.0)
        return _dot((a * a).astype(BF16), wd_ref[...])

    @pl.when(j == 0)
    def _():
        acc_ref[...] = hidden_step()

    @pl.when(j > 0)
    def _():
        acc_ref[...] += hidden_step()

    @pl.when(j == pl.num_programs(1) - 1)
    def _():
        if not final_norm:
            o_ref[...] = x_ref[...] + gate_ref[...] * acc_ref[...]
            return
        gate = gate_ref[...]
        fg = fg_ref[...]

        def body(c, carry):
            rows = pl.ds(pl.multiple_of(c * NORM_ROWS, NORM_ROWS), NORM_ROWS)
            y = x_ref[rows, :] + gate * acc_ref[rows, :]
            ms = jnp.mean(y * y, axis=-1, keepdims=True)
            o_ref[rows, :] = y * lax.rsqrt(ms + NORM_EPS) * fg
            return carry

        lax.fori_loop(0, x_ref.shape[0] // NORM_ROWS, body, 0, unroll=8)


def mlp_block(x, norm_g, mods, row_fn, w_up, w_down, final_g=None, th=1024, tm=ROW_TILE, name="mlp"):
    m, d = x.shape
    hid = w_up.shape[1]
    tm = min(tm, m)
    row_fn = row_fn(tm)
    final_norm = final_g is not None
    fg = (final_g if final_norm else norm_g).reshape(1, d)
    return pl.pallas_call(
        functools.partial(_mlp_kernel, final_norm=final_norm),
        grid=(m // tm, hid // th),
        in_specs=[
            pl.BlockSpec((tm, d), lambda i, j: (i, 0)),
            pl.BlockSpec((1, d), lambda i, j: (0, 0)),
            _mod_spec(row_fn, 3),
            _mod_spec(row_fn, 4),
            _mod_spec(row_fn, 5),
            pl.BlockSpec((d, th), lambda i, j: (0, j)),
            pl.BlockSpec((th, d), lambda i, j: (j, 0)),
            pl.BlockSpec((1, d), lambda i, j: (0, 0)),
        ],
        out_specs=pl.BlockSpec((tm, d), lambda i, j: (i, 0)),
        out_shape=jax.ShapeDtypeStruct((m, d), F32),
        scratch_shapes=[pltpu.VMEM((tm, d), BF16), pltpu.VMEM((tm, d), F32)],
        compiler_params=_cparams("arbitrary", "arbitrary"),
        name=name,
    )(x, norm_g.reshape(1, d), mods, mods, mods, w_up, w_down, fg)


def rope_tables(head_dim):
    rows = SEQ // GRID_W
    row = jnp.repeat(jnp.arange(rows, dtype=F32), GRID_W)
    col = jnp.tile(jnp.arange(GRID_W, dtype=F32), rows)
    axis_dim = head_dim // 2
    inv_freq = jnp.exp(-math.log(ROPE_BASE) * jnp.arange(0, axis_dim, 2, dtype=F32) / axis_dim)
    ang = jnp.concatenate([row[:, None] * inv_freq, col[:, None] * inv_freq], axis=-1)
    cos = jnp.cos(ang)
    sin = jnp.sin(ang)
    return jnp.concatenate([cos, cos], axis=-1), jnp.concatenate([-sin, sin], axis=-1)


def pair_split_weights(w, qk_width, head_dim):
    d = w.shape[0]
    qk = w[:, :qk_width].reshape(d, qk_width // head_dim, head_dim // 2, 2)
    qk = jnp.swapaxes(qk, 2, 3).reshape(d, qk_width)
    return jnp.concatenate([qk, w[:, qk_width:]], axis=1)


def _rope(x, cos, sin_signed):
    partner = pltpu.roll(x, x.shape[-1] // 2, 1)
    return x * cos + partner * sin_signed


def _ret_tables():
    lg = np.log1p(-np.exp2(-5.0 - np.arange(RET_HEADS, dtype=np.float64)))[:, None]
    pos = np.arange(RET_CHUNK, dtype=np.float64)
    dist = np.abs(pos[:, None] - pos[None, :])
    intra = np.exp(lg[:, :, None] * dist) * (1.0 + np.eye(RET_CHUNK))
    q_f = np.exp(lg * (pos + 1.0))
    q_b = np.exp(lg * (RET_CHUNK - pos))
    k_f = np.exp(lg * (RET_CHUNK - 1.0 - pos))
    k_b = np.exp(lg * pos)
    dec = np.stack([q_f, q_b, k_f, k_b], axis=1)
    dec = np.broadcast_to(dec[..., None], dec.shape + (RET_DK,))
    cdec = np.broadcast_to(np.exp(lg * RET_CHUNK)[:, :, None], (RET_HEADS, 8, RET_DV))
    return (jnp.asarray(intra, F32), jnp.asarray(dec, F32), jnp.asarray(cdec, F32))


def _ret_kernel(ql_ref, kl_ref, vl_ref, gl_ref, qc_ref, kc_ref, vc_ref, gc_ref,
                cos_ref, sin_ref, intra_ref, dec_ref, cdec_ref, gn_ref,
                ol_ref, oc_ref, qs_ref, ks_ref, sb_ref, st_ref):
    C = RET_CHUNK
    n_lat = SEQ // C
    n_ctx = CTX_LEN // C
    kscale = RET_DK ** -0.5
    cd = cdec_ref[0:1, :]

    def rope_chunk(c, carry):
        rows = pl.ds(pl.multiple_of(c * C, C), C)
        cs = cos_ref[rows, :]
        sn = sin_ref[rows, :]
        qs_ref[rows, :] = _rope(ql_ref[rows, :], cs, sn)
        ks_ref[rows, :] = _rope(kl_ref[rows, :], cs, sn) * kscale
        return carry

    lax.fori_loop(0, n_lat, rope_chunk, 0)

    st_ref[...] = jnp.zeros_like(st_ref)

    def back_pair(k1, v1, slot1, k2, v2, slot2):
        st = st_ref[...]
        kv1 = _dot_tn((k1 * dec_ref[3]).astype(BF16), v1)
        kv2 = _dot_tn((k2 * dec_ref[3]).astype(BF16), v2)
        sb_ref[slot1] = st.astype(BF16)
        st = st * cd + kv1
        sb_ref[slot2] = st.astype(BF16)
        st_ref[...] = st * cd + kv2

    assert n_ctx == 2 and n_lat % 2 == 0
    back_pair(kc_ref[pl.ds(C, C), :] * kscale, vc_ref[pl.ds(C, C), :], 1,
              kc_ref[pl.ds(0, C), :] * kscale, vc_ref[pl.ds(0, C), :], 0)

    def back_lat(t, carry):
        c1 = n_lat - 1 - 2 * t
        r1 = pl.ds(pl.multiple_of(c1 * C, C), C)
        r2 = pl.ds(pl.multiple_of((c1 - 1) * C, C), C)
        back_pair(ks_ref[r1, :], vl_ref[r1, :], n_ctx + c1, ks_ref[r2, :], vl_ref[r2, :], n_ctx + c1 - 1)
        return carry

    lax.fori_loop(0, n_lat // 2, back_lat, 0)

    st_ref[...] = jnp.zeros_like(st_ref)

    def norm_gate(o, gate):
        ms = jnp.mean(o * o, axis=-1, keepdims=True)
        on = o * lax.rsqrt(ms + NORM_EPS) * gn_ref[...]
        return (on * gate.astype(F32)).astype(BF16)

    def fwd_pair(q1, k1, v1, gate1, slot1, q2, k2, v2, gate2, slot2):
        st = st_ref[...]
        s1 = _dot_nt(q1.astype(BF16), k1.astype(BF16))
        s2 = _dot_nt(q2.astype(BF16), k2.astype(BF16))
        kv1 = _dot_tn((k1 * dec_ref[2]).astype(BF16), v1)
        cross1 = (_dot((q1 * dec_ref[0]).astype(BF16), st.astype(BF16))
                  + _dot((q1 * dec_ref[1]).astype(BF16), sb_ref[slot1]))
        back2 = _dot((q2 * dec_ref[1]).astype(BF16), sb_ref[slot2])
        kv2 = _dot_tn((k2 * dec_ref[2]).astype(BF16), v2)
        o1 = _dot((s1 * intra_ref[...]).astype(BF16), v1) + cross1
        st = st * cd + kv1
        o2 = (_dot((s2 * intra_ref[...]).astype(BF16), v2) + back2
              + _dot((q2 * dec_ref[0]).astype(BF16), st.astype(BF16)))
        st_ref[...] = st * cd + kv2
        return norm_gate(o1, gate1), norm_gate(o2, gate2)

    r1 = pl.ds(0, C)
    r2 = pl.ds(C, C)
    oc_ref[r1, :], oc_ref[r2, :] = fwd_pair(
        qc_ref[r1, :], kc_ref[r1, :] * kscale, vc_ref[r1, :], gc_ref[r1, :], 0,
        qc_ref[r2, :], kc_ref[r2, :] * kscale, vc_ref[r2, :], gc_ref[r2, :], 1)

    def fwd_lat(t, carry):
        r1 = pl.ds(pl.multiple_of(2 * t * C, C), C)
        r2 = pl.ds(pl.multiple_of((2 * t + 1) * C, C), C)
        ol_ref[r1, :], ol_ref[r2, :] = fwd_pair(
            qs_ref[r1, :], ks_ref[r1, :], vl_ref[r1, :], gl_ref[r1, :], n_ctx + 2 * t,
            qs_ref[r2, :], ks_ref[r2, :], vl_ref[r2, :], gl_ref[r2, :], n_ctx + 2 * t + 1)
        return carry

    lax.fori_loop(0, n_lat // 2, fwd_lat, 0)


def retention_core(qk_lat, v_lat, g_lat, qk_ctx, v_ctx, g_ctx, gn_g, batch):
    H = RET_HEADS
    cos, sin = rope_tables(RET_DK)
    intra, dec, cdec = _ret_tables()
    n_chunks = (SEQ + CTX_LEN) // RET_CHUNK

    def specs(rows):
        return [
            pl.BlockSpec((rows, RET_DK), lambda b, h: (b, h)),
            pl.BlockSpec((rows, RET_DK), lambda b, h: (b, H + h)),
            pl.BlockSpec((rows, RET_DV), lambda b, h: (b, h)),
            pl.BlockSpec((rows, RET_DV), lambda b, h: (b, h)),
        ]

    return pl.pallas_call(
        _ret_kernel,
        grid=(batch, H),
        in_specs=specs(SEQ) + specs(CTX_LEN) + [
            pl.BlockSpec((SEQ, RET_DK), lambda b, h: (0, 0)),
            pl.BlockSpec((SEQ, RET_DK), lambda b, h: (0, 0)),
            pl.BlockSpec((None, RET_CHUNK, RET_CHUNK), lambda b, h: (h, 0, 0)),
            pl.BlockSpec((None, 4, RET_CHUNK, RET_DK), lambda b, h: (h, 0, 0, 0)),
            pl.BlockSpec((None, 8, RET_DV), lambda b, h: (h, 0, 0)),
            pl.BlockSpec((1, RET_DV), lambda b, h: (0, h)),
        ],
        out_specs=[
            pl.BlockSpec((SEQ, RET_DV), lambda b, h: (b, h)),
            pl.BlockSpec((CTX_LEN, RET_DV), lambda b, h: (b, h)),
        ],
        out_shape=[
            jax.ShapeDtypeStruct((batch * SEQ, H * RET_DV), BF16),
            jax.ShapeDtypeStruct((batch * CTX_LEN, H * RET_DV), BF16),
        ],
        scratch_shapes=[
            pltpu.VMEM((SEQ, RET_DK), F32),
            pltpu.VMEM((SEQ, RET_DK), F32),
            pltpu.VMEM((n_chunks, RET_DK, RET_DV), BF16),
            pltpu.VMEM((RET_DK, RET_DV), F32),
        ],
        compiler_params=_cparams("arbitrary", "arbitrary"),
        name="retention_core",
    )(qk_lat, qk_lat, v_lat, g_lat, qk_ctx, qk_ctx, v_ctx, g_ctx,
      cos, sin, intra, dec, cdec, gn_g.reshape(1, H * RET_DV))


def _att_kernel(sink_ref, ql_ref, kl_ref, vl_ref, qc_ref, kc_ref, vc_ref, cos_ref, sin_ref,
                ol_ref, oc_ref, kp_ref, vp_ref):
    BLK = ATT_BLOCK
    G = ATT_GROUP
    HD = ATT_HD
    n_blk = SEQ // BLK
    span = BLK + 2 * WINDOW
    scale = HD ** -0.5
    kvh = pl.program_id(1)

    zpad = jnp.zeros((WINDOW, HD), BF16)
    kp_ref[pl.ds(0, WINDOW), :] = zpad
    vp_ref[pl.ds(0, WINDOW), :] = zpad
    kp_ref[pl.ds(WINDOW + SEQ, WINDOW), :] = zpad
    vp_ref[pl.ds(WINDOW + SEQ, WINDOW), :] = zpad
    ctx0 = SEQ + 2 * WINDOW
    kp_ref[pl.ds(ctx0, CTX_LEN), :] = kc_ref[...].astype(BF16)
    vp_ref[pl.ds(ctx0, CTX_LEN), :] = vc_ref[...]

    def stage(c, carry):
        rows = pl.ds(pl.multiple_of(c * BLK, BLK), BLK)
        dst = pl.ds(pl.multiple_of(c * BLK + WINDOW, BLK), BLK)
        kp_ref[dst, :] = _rope(kl_ref[rows, :], cos_ref[rows, :], sin_ref[rows, :]).astype(BF16)
        vp_ref[dst, :] = vl_ref[rows, :]
        return carry

    lax.fori_loop(0, n_blk, stage, 0)

    kc_b = kp_ref[pl.ds(ctx0, CTX_LEN), :]
    vc_b = vp_ref[pl.ds(ctx0, CTX_LEN), :]

    def finish(parts, out_ref, rows, mask=None):
        pending = []
        for g in range(G):
            sink = jnp.full((BLK, 1), sink_ref[kvh * G + g], F32)
            logits = []
            for i, (s, _) in enumerate(parts):
                sg = s[g * BLK:(g + 1) * BLK, :]
                logits.append(jnp.where(mask, sg, NEG_BIG) if (mask is not None and i == 0) else sg)
            m = sink
            for sg in logits:
                m = jnp.maximum(m, jnp.max(sg, axis=-1, keepdims=True))
            den = jnp.exp(sink - m)
            acc = None
            for sg, (_, v) in zip(logits, parts):
                p = jnp.exp(sg - m)
                den = den + jnp.sum(p, axis=-1, keepdims=True)
                pv = _dot(p.astype(BF16), v)
                acc = pv if acc is None else acc + pv
            pending.append((acc, den))
        for g, (acc, den) in enumerate(pending):
            out_ref[rows, pl.ds(g * HD, HD)] = (acc / den).astype(BF16)

    qi = lax.broadcasted_iota(jnp.int32, (BLK, span), 0)
    col = lax.broadcasted_iota(jnp.int32, (BLK, span), 1)
    band = (col >= qi) & (col <= qi + 2 * WINDOW)

    def lat_block(bi, carry):
        rows = pl.ds(pl.multiple_of(bi * BLK, BLK), BLK)
        cs = cos_ref[rows, :]
        sn = sin_ref[rows, :]
        qs = [(_rope(ql_ref[rows, pl.ds(g * HD, HD)], cs, sn) * scale).astype(BF16) for g in range(G)]
        q4 = jnp.concatenate(qs, axis=0)
        krows = pl.ds(pl.multiple_of(bi * BLK, BLK), span)
        s_lat = _dot_nt(q4, kp_ref[krows, :])
        kpos = col + (bi * BLK - WINDOW)
        valid = band & (kpos >= 0) & (kpos < SEQ)
        s_ctx = _dot_nt(q4, kc_b)
        finish([(s_lat, vp_ref[krows, :]), (s_ctx, vc_b)], ol_ref, rows, mask=valid)
        return carry

    lax.fori_loop(0, n_blk, lat_block, 0)

    for bi in range(CTX_LEN // BLK):
        rows = pl.ds(bi * BLK, BLK)
        qs = [(qc_ref[rows, pl.ds(g * HD, HD)] * scale).astype(BF16) for g in range(G)]
        q4 = jnp.concatenate(qs, axis=0)
        finish([(_dot_nt(q4, kc_b), vc_b)], oc_ref, rows)


def attention_core(q_lat, k_lat, v_lat, q_ctx, k_ctx, v_ctx, sink, batch):
    cos, sin = rope_tables(ATT_HD)
    GW = ATT_GROUP * ATT_HD
    pad_rows = SEQ + 2 * WINDOW + CTX_LEN

    def specs(rows):
        return [
            pl.BlockSpec((rows, GW), lambda b, h: (b, h)),
            pl.BlockSpec((rows, ATT_HD), lambda b, h: (b, h)),
            pl.BlockSpec((rows, ATT_HD), lambda b, h: (b, h)),
        ]

    return pl.pallas_call(
        _att_kernel,
        grid=(batch, ATT_KV_HEADS),
        in_specs=[pl.BlockSpec(memory_space=pltpu.SMEM)] + specs(SEQ) + specs(CTX_LEN) + [
            pl.BlockSpec((SEQ, ATT_HD), lambda b, h: (0, 0)),
            pl.BlockSpec((SEQ, ATT_HD), lambda b, h: (0, 0)),
        ],
        out_specs=[
            pl.BlockSpec((SEQ, GW), lambda b, h: (b, h)),
            pl.BlockSpec((CTX_LEN, GW), lambda b, h: (b, h)),
        ],
        out_shape=[
            jax.ShapeDtypeStruct((batch * SEQ, ATT_HEADS * ATT_HD), BF16),
            jax.ShapeDtypeStruct((batch * CTX_LEN, ATT_HEADS * ATT_HD), BF16),
        ],
        scratch_shapes=[
            pltpu.VMEM((pad_rows, ATT_HD), BF16),
            pltpu.VMEM((pad_rows, ATT_HD), BF16),
        ],
        compiler_params=_cparams("arbitrary", "arbitrary"),
        name="attention_core",
    )(sink.astype(F32), q_lat, k_lat, v_lat, q_ctx, k_ctx, v_ctx, cos, sin)


def _sgu_kernel(u_ref, v_ref, g_ref, b_ref, ws_ref, bs_ref, o_ref, *, chunks):
    C = SGU_CHUNK
    gw = SGU_WIDTH // SGU_GROUPS
    for c in range(chunks):
        rows = pl.ds(c * C, C)
        v = v_ref[rows, :]
        mu = jnp.mean(v, axis=-1, keepdims=True)
        xc = v - mu
        var = jnp.mean(xc * xc, axis=-1, keepdims=True)
        vn = (xc * lax.rsqrt(var + NORM_EPS) * g_ref[...] + b_ref[...]).astype(BF16)
        for g in range(SGU_GROUPS):
            cols = pl.ds(g * gw, gw)
            t = _dot(ws_ref[g], vn[:, g * gw:(g + 1) * gw]) + bs_ref[g]
            o_ref[rows, cols] = (u_ref[rows, cols].astype(F32) * t).astype(BF16)


def sgu_core(u, v, ln_g, ln_b, w_s, b_s, chunks=2):
    m = u.shape[0]
    tm = chunks * SGU_CHUNK
    return pl.pallas_call(
        functools.partial(_sgu_kernel, chunks=chunks),
        grid=(m // tm,),
        in_specs=[
            pl.BlockSpec((tm, SGU_WIDTH), lambda i: (i, 0)),
            pl.BlockSpec((tm, SGU_WIDTH), lambda i: (i, 0)),
            pl.BlockSpec((1, SGU_WIDTH), lambda i: (0, 0)),
            pl.BlockSpec((1, SGU_WIDTH), lambda i: (0, 0)),
            pl.BlockSpec((SGU_GROUPS, SGU_CHUNK, SGU_CHUNK), lambda i: (0, 0, 0)),
            pl.BlockSpec((SGU_GROUPS, SGU_CHUNK, 1), lambda i: (0, 0, 0)),
        ],
        out_specs=pl.BlockSpec((tm, SGU_WIDTH), lambda i: (i, 0)),
        out_shape=jax.ShapeDtypeStruct((m, SGU_WIDTH), BF16),
        compiler_params=_cparams("arbitrary"),
        name="sgu_core",
    )(u, v, ln_g.reshape(1, -1), ln_b.reshape(1, -1), w_s.astype(BF16),
      b_s.reshape(SGU_GROUPS, SGU_CHUNK, 1))


GDN_ROWS = CTX_LEN + SEQ
GDN_NCHUNK = GDN_ROWS // GDN_CHUNK
GDN_CONV_ROWS = 256
GDN_HALO = 8
GDN_PREP_TILES = 3


def _gdn_kernel(ql_ref, kl_ref, vl_ref, zl_ref, qc_ref, kc_ref, vc_ref, ba_ref,
                cwq_ref, cwk_ref, cwv_ref, alog_ref, dt_ref, ng_ref,
                ol_ref,
                xp_ref, qn_ref, kn_ref, vv_ref, gt_ref, qp_ref, g_ref, b_ref,
                el_ref, of_ref, ob_ref, s_ref):
    C = GDN_CHUNK
    DK = GDN_DK
    R = GDN_CONV_ROWS
    HALO = GDN_HALO

    def conv_silu(src_ref, n_rows, cw_ref, width, store):
        cols = pl.ds(0, width)
        xp_ref[pl.ds(0, HALO), cols] = jnp.zeros((HALO, width), F32)
        xp_ref[pl.ds(HALO, n_rows), cols] = src_ref[...]
        xp_ref[pl.ds(HALO + n_rows, HALO), cols] = jnp.zeros((HALO, width), F32)
        w = cw_ref[...]
        n = R + 2 * HALO

        def body(c, carry):
            r0 = pl.multiple_of(c * R, R)
            x = xp_ref[pl.ds(r0, n), cols]
            y = (w[2:3] * x + w[1:2] * pltpu.roll(x, 1, 0) + w[0:1] * pltpu.roll(x, 2, 0)
                 + w[3:4] * pltpu.roll(x, n - 1, 0))[HALO:HALO + R]
            store(r0, _silu(y))
            return carry

        lax.fori_loop(0, n_rows // R, body, 0)

    def unit(y):
        return y * lax.rsqrt(jnp.sum(y * y, axis=-1, keepdims=True) + 1e-6)

    def store_q(off):
        def f(r0, y):
            qn_ref[pl.ds(pl.multiple_of(off + r0, R), R), :] = unit(y) * (DK ** -0.5)
        return f

    def store_k(off):
        def f(r0, y):
            kn_ref[pl.ds(pl.multiple_of(off + r0, R), R), :] = unit(y)
        return f

    def store_v(off):
        def f(r0, y):
            vv_ref[pl.ds(pl.multiple_of(off + r0, R), R), :] = y
        return f

    conv_silu(qc_ref, CTX_LEN, cwq_ref, DK, store_q(0))
    conv_silu(kc_ref, CTX_LEN, cwk_ref, DK, store_k(0))
    conv_silu(vc_ref, CTX_LEN, cwv_ref, 2 * GDN_DV, store_v(0))
    conv_silu(ql_ref, SEQ, cwq_ref, DK, store_q(CTX_LEN))
    conv_silu(kl_ref, SEQ, cwk_ref, DK, store_k(CTX_LEN))
    conv_silu(vl_ref, SEQ, cwv_ref, 2 * GDN_DV, store_v(CTX_LEN))

    raw = ba_ref[...]
    xg = raw + dt_ref[...]
    softplus = jnp.maximum(xg, 0.0) + jnp.log1p(jnp.exp(-jnp.abs(xg)))
    g = -jnp.exp(alog_ref[...]) * softplus
    pos = lax.broadcasted_iota(jnp.int32, raw.shape, 1) % C
    pre = g
    suf = g
    shift = 1
    while shift < C:
        pre = pre + jnp.where(pos >= shift, pltpu.roll(pre, shift, 1), 0.0)
        suf = suf + jnp.where(pos + shift < C, pltpu.roll(suf, GDN_ROWS - shift, 1), 0.0)
        shift *= 2
    row = lax.broadcasted_iota(jnp.int32, raw.shape, 0)
    gt_ref[...] = jnp.where(row < 4, jax.nn.sigmoid(raw), jnp.where(row < 6, pre, suf))

    PK = 2 * C
    ri = lax.broadcasted_iota(jnp.int32, (C, PK), 0)
    li = lax.broadcasted_iota(jnp.int32, (C, PK), 1)
    ci = li % C
    left = li < C
    eye = (ri == ci).astype(F32)
    incl = (ri >= ci, ri <= ci)
    strict = (ri > ci, ri < ci)
    half_sel = (li == ri, li == ri + C)

    def to_col(rowvec, half):
        return jnp.sum(jnp.where(half_sel[half], rowvec, 0.0), axis=1, keepdims=True)

    def block_diag(m, zero):
        return jnp.concatenate([jnp.concatenate([m[0], zero], axis=1),
                                jnp.concatenate([zero, m[1]], axis=1)], axis=0)

    def dot3_packed(x, m):
        bd = jnp.concatenate([jnp.where(left, m, 0.0), jnp.where(left, 0.0, m)], axis=0)
        x_hi = x.astype(BF16)
        bd_hi = bd.astype(BF16)
        x_lo = (x - x_hi.astype(F32)).astype(BF16)
        bd_lo = (bd - bd_hi.astype(F32)).astype(BF16)
        rhs = jnp.concatenate([jnp.concatenate([bd_hi, bd_lo], axis=1),
                               jnp.concatenate([bd_hi, jnp.zeros_like(bd_hi)], axis=1)], axis=0)
        out = _dot(jnp.concatenate([x_hi, x_lo], axis=1), rhs)
        return out[:, :PK] + out[:, PK:]

    def prep_pair(p, carry):
        zero_uw = jnp.zeros((C, 2 * GDN_DV), BF16)
        chains = []
        for sub in range(2 * GDN_PREP_TILES):
            tile, half = divmod(sub, 2)
            if half == 0:
                lanes = pl.ds(pl.multiple_of((p * GDN_PREP_TILES + tile) * PK, PK), PK)
                gates = gt_ref[:, lanes]
                gates_sw = pltpu.roll(gates, C, 1)
            chunk = (p * GDN_PREP_TILES + tile) * 2 + half
            rows = pl.ds(pl.multiple_of(chunk * C, C), C)
            q = qn_ref[rows, :]
            k = kn_ref[rows, :]
            v2 = vv_ref[rows, :]
            qk_kk = _dot_nt(jnp.concatenate([q, k], axis=0).astype(BF16),
                            jnp.concatenate([k, k], axis=0).astype(BF16))
            for d in range(2):
                def packed_row(r, half=half, d=d):
                    a = (gates if half == 0 else gates_sw)[r + 2 * d:r + 2 * d + 1]
                    b = (gates_sw if half == 0 else gates)[r + 2 * d + 1:r + 2 * d + 2]
                    return jnp.where(left[0:1], a, b)
                edge = half * C + (C - 1 if d == 0 else 0)
                chains.append(dict(
                    d=d, n=chunk, rows=rows, q=q, k=k, v2=v2, qk=qk_kk[:C], kk=qk_kk[C:],
                    grow=packed_row(4),
                    bcols=[to_col(gates[2 * d + j:2 * d + j + 1], half) for j in range(2)],
                    gcols=[to_col(gates[4 + 2 * d + j:5 + 2 * d + j], half) for j in range(2)],
                    glast=[gates[4 + 2 * d + j:5 + 2 * d + j, edge:edge + 1] for j in range(2)]))
        for ch in chains:
            d = ch["d"]
            bcol = jnp.where(left, ch["bcols"][0], ch["bcols"][1])
            gcol = jnp.where(left, ch["gcols"][0], ch["gcols"][1])
            ch["dec"] = jnp.exp(jnp.where(incl[d], gcol - ch["grow"], NEG_BIG))
            nmat = jnp.where(strict[d], -(ch["kk"] * bcol * ch["dec"]), 0.0)
            ch["pm"] = eye + nmat
            ch["nmat"] = nmat
        for ch in chains:
            ch["mm"] = dot3_packed(ch["nmat"], ch["nmat"])
        for _ in range(4):
            for ch in chains:
                r = dot3_packed(jnp.concatenate([ch["pm"], ch["mm"]], axis=0), ch["mm"])
                ch["pm"] = ch["pm"] + r[:C]
                ch["mm"] = r[C:]
        for ch in chains:
            ch["r"] = dot3_packed(ch["pm"], ch["mm"])
        for ch in chains:
            tmat = (ch["pm"] + ch["r"]).astype(BF16)
            ch["eg"] = [jnp.exp(g) for g in ch["gcols"]]
            rhs = [jnp.concatenate([ch["v2"][:, j * GDN_DV:(j + 1) * GDN_DV] * ch["bcols"][j],
                                    ch["k"] * (ch["bcols"][j] * ch["eg"][j])], axis=1).astype(BF16)
                   for j in range(2)]
            ch["uw"] = _dot(tmat, block_diag(rhs, zero_uw)).astype(BF16)
        for ch in chains:
            uw = [ch["uw"][:, j * 2 * GDN_DV:(j + 1) * 2 * GDN_DV] for j in range(2)]
            amat = (ch["qk"] * ch["dec"]).astype(BF16)
            ch["a_uw"] = _dot(amat, block_diag(uw, zero_uw))
            ch["k_uw"] = [_dot_tn((ch["k"] * jnp.exp(ch["glast"][j] - ch["gcols"][j])).astype(BF16), uw[j])
                          for j in range(2)]
        for ch in chains:
            n, rows = ch["n"], ch["rows"]
            out_ref = of_ref if ch["d"] == 0 else ob_ref
            for j in range(2):
                combo = 2 * ch["d"] + j
                a_u = ch["a_uw"][:, (2 * j) * GDN_DV:(2 * j + 1) * GDN_DV]
                a_w = ch["a_uw"][:, (2 * j + 1) * GDN_DV:(2 * j + 2) * GDN_DV]
                qp_ref[combo, rows, :] = (ch["q"] * ch["eg"][j] - a_w).astype(BF16)
                out_ref[rows, pl.ds(j * GDN_DV, GDN_DV)] = a_u
                b_ref[combo, n] = ch["k_uw"][j][:, :GDN_DV]
                g_ref[combo, n] = ch["k_uw"][j][:, GDN_DV:].astype(BF16)
                el_ref[combo, n] = jnp.broadcast_to(jnp.exp(ch["glast"][j]), (8, GDN_DV))
        return carry

    lax.fori_loop(0, GDN_NCHUNK // (2 * GDN_PREP_TILES), prep_pair, 0)

    s_ref[...] = jnp.zeros_like(s_ref)

    def rec_pair(nf, nb):
        work = []
        for combo in range(4):
            n = nf if combo < 2 else nb
            rows = pl.ds(pl.multiple_of(n * C, C), C)
            out_ref = of_ref if combo < 2 else ob_ref
            cols = pl.ds((combo % 2) * GDN_DV, GDN_DV)
            lhs = jnp.concatenate([g_ref[combo, n], qp_ref[combo, rows, :]], axis=0)
            work.append((combo, out_ref, rows, cols, lhs, s_ref[combo], b_ref[combo, n],
                         el_ref[combo, n][0:1, :], out_ref[rows, cols]))
        done = []
        for combo, out_ref, rows, cols, lhs, st, bmat, e_last, o0 in work:
            r = _dot(lhs, st.astype(BF16))
            done.append((combo, out_ref, rows, cols, st * e_last + bmat - r[:GDN_DK], o0 + r[GDN_DK:]))
        for combo, out_ref, rows, cols, st_new, o in done:
            s_ref[combo] = st_new
            out_ref[rows, cols] = o

    n_ctx = CTX_LEN // C
    for t in range(n_ctx):
        rec_pair(t, n_ctx - 1 - t)

    def rec_lat(t, carry):
        rec_pair(n_ctx + t, GDN_NCHUNK - 1 - t)
        return carry

    lax.fori_loop(0, SEQ // C, rec_lat, 0)

    def fin(c, carry):
        src = pl.ds(pl.multiple_of(CTX_LEN + c * R, R), R)
        dst = pl.ds(pl.multiple_of(c * R, R), R)
        for j in range(2):
            cols = pl.ds(j * GDN_DV, GDN_DV)
            o = of_ref[src, cols] + ob_ref[src, cols]
            ms = jnp.mean(o * o, axis=-1, keepdims=True)
            on = o * lax.rsqrt(ms + NORM_EPS) * ng_ref[...]
            ol_ref[dst, cols] = (on * zl_ref[dst, cols].astype(F32)).astype(BF16)
        return carry

    lax.fori_loop(0, SEQ // R, fin, 0)


def gdn_core(mixed_lat, z_lat, mixed_ctx, ba, conv_w, a_log, dt_bias, norm_g, batch):
    HK = GDN_K_HEADS
    DK = GDN_DK
    VW = 2 * GDN_DV
    v_blk0 = (2 * HK * DK) // VW

    def tab(t):
        t = t.reshape(2, HK, 2).transpose(1, 0, 2).reshape(HK, 4)
        return jnp.concatenate([jnp.zeros_like(t), t], axis=1).reshape(HK, 8, 1).astype(F32)

    def specs(rows):
        return [
            pl.BlockSpec((rows, DK), lambda b, h: (b, h)),
            pl.BlockSpec((rows, DK), lambda b, h: (b, HK + h)),
            pl.BlockSpec((rows, VW), lambda b, h: (b, v_blk0 + h)),
        ]

    nc = GDN_NCHUNK
    return pl.pallas_call(
        _gdn_kernel,
        grid=(batch, HK),
        in_specs=specs(SEQ) + [pl.BlockSpec((SEQ, VW), lambda b, h: (b, h))] + specs(CTX_LEN) + [
            pl.BlockSpec((None, None, 8, GDN_ROWS), lambda b, h: (b, h, 0, 0)),
            pl.BlockSpec((GDN_CONV, DK), lambda b, h: (0, h)),
            pl.BlockSpec((GDN_CONV, DK), lambda b, h: (0, HK + h)),
            pl.BlockSpec((GDN_CONV, VW), lambda b, h: (0, v_blk0 + h)),
            pl.BlockSpec((None, 8, 1), lambda b, h: (h, 0, 0)),
            pl.BlockSpec((None, 8, 1), lambda b, h: (h, 0, 0)),
            pl.BlockSpec((1, GDN_DV), lambda b, h: (0, 0)),
        ],
        out_specs=pl.BlockSpec((SEQ, VW), lambda b, h: (b, h)),
        out_shape=jax.ShapeDtypeStruct((batch * SEQ, GDN_V_HEADS * GDN_DV), BF16),
        scratch_shapes=[
            pltpu.VMEM((SEQ + 2 * GDN_HALO, VW), F32),
            pltpu.VMEM((GDN_ROWS, DK), F32),
            pltpu.VMEM((GDN_ROWS, DK), F32),
            pltpu.VMEM((GDN_ROWS, VW), F32),
            pltpu.VMEM((8, GDN_ROWS), F32),
            pltpu.VMEM((4, GDN_ROWS, DK), BF16),
            pltpu.VMEM((4, nc, DK, GDN_DV), BF16),
            pltpu.VMEM((4, nc, DK, GDN_DV), F32),
            pltpu.VMEM((4, nc, 8, GDN_DV), F32),
            pltpu.VMEM((GDN_ROWS, VW), F32),
            pltpu.VMEM((GDN_ROWS, VW), F32),
            pltpu.VMEM((4, DK, GDN_DV), F32),
        ],
        compiler_params=_cparams("arbitrary", "arbitrary"),
        name="gdn_core",
    )(mixed_lat, mixed_lat, mixed_lat, z_lat, mixed_ctx, mixed_ctx, mixed_ctx, ba,
      conv_w, conv_w, conv_w, tab(a_log), tab(dt_bias), norm_g.reshape(1, GDN_DV))


def _finish_layer(o_lat, o_ctx, xl, xc, w_out, norm2_g, up, down, mods, rows, final_g=None):
    lat_row, ctx_row = rows
    w_out = w_out.astype(BF16)
    up = up.astype(BF16)
    down = down.astype(BF16)
    xl = linear_post(o_lat, w_out, xl, mods, lat_row, 2, name="out_proj_lat")
    xl = mlp_block(xl, norm2_g, mods, lat_row, up, down, final_g=final_g, name="mlp_lat")
    if o_ctx is not None:
        xc = linear_post(o_ctx, w_out, xc, mods, ctx_row, 2, name="out_proj_ctx")
        xc = mlp_block(xc, norm2_g, mods, ctx_row, up, down, name="mlp_ctx")
    return xl, xc


def retention_layer(xl, xc, mods, rows, batch, norm1_g, norm2_g, w_in, gn_g, w_out, up, down):
    lat_row, ctx_row = rows
    qkw = 2 * RET_HEADS * RET_DK
    vw = RET_HEADS * RET_DV
    w_in = pair_split_weights(w_in.astype(BF16), qkw, RET_DK)
    segs = [(qkw, F32, _identity), (vw, BF16, _identity), (vw, BF16, _silu)]
    qk_l, v_l, g_l = linear_pre(xl, norm1_g, mods, lat_row, w_in, segs, name="ret_in_lat")
    qk_c, v_c, g_c = linear_pre(xc, norm1_g, mods, ctx_row, w_in, segs, name="ret_in_ctx")
    o_l, o_c = retention_core(qk_l, v_l, g_l, qk_c, v_c, g_c, gn_g, batch)
    return _finish_layer(o_l, o_c, xl, xc, w_out, norm2_g, up, down, mods, rows)


def attention_layer(xl, xc, mods, rows, batch, norm1_g, norm2_g, w_in, sink, w_out, up, down):
    lat_row, ctx_row = rows
    qw = ATT_HEADS * ATT_HD
    kvw = ATT_KV_HEADS * ATT_HD
    w_in = pair_split_weights(w_in.astype(BF16), qw + kvw, ATT_HD)
    segs = [(qw, F32, _identity), (kvw, F32, _identity), (kvw, BF16, _identity)]
    q_l, k_l, v_l = linear_pre(xl, norm1_g, mods, lat_row, w_in, segs, name="att_in_lat")
    q_c, k_c, v_c = linear_pre(xc, norm1_g, mods, ctx_row, w_in, segs, name="att_in_ctx")
    o_l, o_c = attention_core(q_l, k_l, v_l, q_c, k_c, v_c, sink, batch)
    return _finish_layer(o_l, o_c, xl, xc, w_out, norm2_g, up, down, mods, rows)


def gdn_layer(xl, xc, mods, rows, batch, norm1_g, norm2_g, w_in, conv_w, a_log, dt_bias, norm_g,
              w_out, up, down):
    lat_row, ctx_row = rows
    conv_ch = 2 * GDN_K_HEADS * GDN_DK + GDN_V_HEADS * GDN_DV
    zw = GDN_V_HEADS * GDN_DV
    w_in = w_in.astype(BF16)
    gate_w = 4 * GDN_V_HEADS
    gate0 = conv_ch + zw
    segs = [(conv_ch, F32, _identity), (zw, BF16, _silu)]
    gsegs = [(gate_w, F32, _identity)]
    mx_l, z_l = linear_pre(xl, norm1_g, mods, lat_row, w_in, segs, name="gdn_in_lat")
    mx_c, _ = linear_pre(xc, norm1_g, mods, ctx_row, w_in, segs, name="gdn_in_ctx")
    (ba_l,) = linear_pre(xl, norm1_g, mods, lat_row, w_in, gsegs, tn=gate_w, col0=gate0, name="gdn_gate_lat")
    (ba_c,) = linear_pre(xc, norm1_g, mods, ctx_row, w_in, gsegs, tn=gate_w, col0=gate0, name="gdn_gate_ctx")
    ba = jnp.concatenate([ba_c.reshape(batch, CTX_LEN, -1), ba_l.reshape(batch, SEQ, -1)], axis=1)
    ba = ba.reshape(batch, GDN_ROWS, 2, 2, GDN_K_HEADS, 2).transpose(0, 4, 2, 3, 5, 1)
    ba = ba.reshape(batch, GDN_K_HEADS, 8, GDN_ROWS)
    o_l = gdn_core(mx_l, z_l, mx_c, ba, conv_w, a_log, dt_bias, norm_g, batch)
    return _finish_layer(o_l, None, xl, xc, w_out, norm2_g, up, down, mods, rows)


def sgu_layer(xl, xc, mods, rows, norm1_g, norm2_g, w_in, ln_g, ln_b, w_s, b_s, w_out, up, down,
              final_g):
    lat_row, _ = rows
    segs = [(SGU_WIDTH, BF16, _gelu_erf), (SGU_WIDTH, F32, _gelu_erf)]
    u_l, v_l = linear_pre(xl, norm1_g, mods, lat_row, w_in.astype(BF16), segs, name="sgu_in_lat")
    o_l = sgu_core(u_l, v_l, ln_g, ln_b, w_s, b_s)
    return _finish_layer(o_l, None, xl, xc, w_out, norm2_g, up, down, mods, rows, final_g=final_g)


def kernel(x, c, ctx, c_ctx, mod_w, mod_b, norm1_g, norm2_g, mlp_up, mlp_down, final_g,
           ret_w_in, ret_gn_g, ret_w_out, att_w_in, att_sink, att_w_out,
           gdn_w_in, gdn_conv_w, gdn_a_log, gdn_dt_bias, gdn_norm_g, gdn_w_out,
           sgu_w_in, sgu_ln_g, sgu_ln_b, sgu_w_s, sgu_b_s, sgu_w_out):
    batch, seq, d = x.shape
    assert (seq, d, ctx.shape[1]) == (SEQ, D_MODEL, CTX_LEN) and batch < MOD_ROWS
    xl = x.reshape(batch * SEQ, d)
    xc = ctx.reshape(batch * CTX_LEN, d)

    cond = jnp.zeros((MOD_ROWS, d), F32).at[:batch].set(c).at[batch].set(c_ctx)
    mods_all = ada_mods(cond, mod_w, mod_b).reshape(DEPTH, MOD_ROWS, 1, 6 * d)
    rows = (functools.partial(_row_fn, SEQ, False, batch),
            functools.partial(_row_fn, CTX_LEN, True, batch))

    xl, xc = retention_layer(xl, xc, mods_all[0], rows, batch, norm1_g[0], norm2_g[0],
                             ret_w_in[0], ret_gn_g[0], ret_w_out[0], mlp_up[0], mlp_down[0])
    xl, xc = attention_layer(xl, xc, mods_all[1], rows, batch, norm1_g[1], norm2_g[1],
                             att_w_in[0], att_sink[0], att_w_out[0], mlp_up[1], mlp_down[1])
    xl, xc = gdn_layer(xl, xc, mods_all[2], rows, batch, norm1_g[2], norm2_g[2],
                       gdn_w_in[0], gdn_conv_w[0], gdn_a_log[0], gdn_dt_bias[0], gdn_norm_g[0],
                       gdn_w_out[0], mlp_up[2], mlp_down[2])
    xl, xc = sgu_layer(xl, xc, mods_all[3], rows, norm1_g[3], norm2_g[3],
                       sgu_w_in[0], sgu_ln_g[0], sgu_ln_b[0], sgu_w_s[0], sgu_b_s[0], sgu_w_out[0],
                       mlp_up[3], mlp_down[3], final_g)
    return xl.reshape(batch, SEQ, d)
```
